```python
import jax, jax.numpy as jnp
from jax import lax
import numpy as np

D_MODEL = 2048
BATCH = 4
SEQ = 2048
DEPTH = 1
DEC_BATCH = 128
DEC_SEQ = 8
PAST_LEN = 16384
PAGE_SIZE = 128

POOL_WIDTH = D_MODEL // 2
POOL_GROUPS = 4
POOL_GROUP_DIM = POOL_WIDTH // POOL_GROUPS
POOL_WINDOWS = (2, 4, 8, 16)
POOL_BUF = max(POOL_WINDOWS) - 1
RET_WIDTH = D_MODEL - POOL_WIDTH
RET_HEADS = 4
RET_HEAD_DIM = RET_WIDTH // RET_HEADS
RET_CHUNK = 128
ROPE_BASE = 10000.0
IN_PROJ_WIDTH = POOL_WIDTH + 4 * RET_WIDTH
MEM_TOKENS = 256
MEM_HEADS = 4
MEM_HEAD_DIM = D_MODEL // MEM_HEADS
PEER_HEADS = 8
PEER_N_KEYS = 128
PEER_N_EXPERTS = PEER_N_KEYS * PEER_N_KEYS
PEER_TOPK = 16
PEER_QUERY_DIM = 256
PEER_HALF = PEER_QUERY_DIM // 2
PEER_BLOCK = 128
NORM_EPS = 1e-6

kernel_name = "hymba_pool_retnet_peer_decoder_step"

F32 = jnp.float32


def rmsnorm(x, g):
    xf = x.astype(F32)
    y = xf * lax.rsqrt(jnp.mean(jnp.square(xf), axis=-1, keepdims=True) + NORM_EPS)
    return (y * g.astype(F32)).astype(x.dtype)


def pool_mix(p_ext, pos, pool_w, pool_scale):
    B, L, W = p_ext.shape
    T = L - POOL_BUF
    pf = p_ext.astype(F32)
    c = jnp.pad(jnp.cumsum(pf, axis=1), ((0, 0), (1, 0), (0, 0)))
    end = c[:, POOL_BUF + 1:]
    means = []
    for gi, w in enumerate(POOL_WINDOWS):
        sl = slice(gi * POOL_GROUP_DIM, (gi + 1) * POOL_GROUP_DIM)
        start = c[:, POOL_BUF + 1 - w: POOL_BUF + 1 - w + T, sl]
        cnt = jnp.minimum(pos + 1, w).astype(F32)[None, :, None]
        means.append((end[..., sl] - start) / cnt)
    d = jnp.concatenate(means, axis=-1) - pf[:, POOL_BUF:]
    d = d.reshape(B, T, POOL_GROUPS, POOL_GROUP_DIM)
    y = jnp.einsum('btgc,gcd->btgd', d, pool_w.astype(F32)).reshape(B, T, W)
    return y * pool_scale.astype(F32)


def rotary(x, pos):
    half = x.shape[-1] // 2
    inv = 1.0 / (ROPE_BASE ** (jnp.arange(half, dtype=F32) / half))
    ang = pos.astype(F32)[:, None] * inv[None, :]
    cos, sin = jnp.cos(ang), jnp.sin(ang)
    x1, x2 = x[..., :half], x[..., half:]
    return jnp.concatenate([x1 * cos - x2 * sin, x1 * sin + x2 * cos], axis=-1)


def ret_log_gamma():
    return jnp.log(1.0 - 2.0 ** (-5.0 - jnp.arange(RET_HEADS, dtype=F32)))


def retention_chunk(q, k, v, s, log_gamma):
    C = q.shape[2]
    idx = jnp.arange(C, dtype=F32)
    diff = idx[:, None] - idx[None, :]
    lg = log_gamma[:, None, None]
    decay = jnp.where(diff >= 0, jnp.exp(lg * jnp.maximum(diff, 0.0)), 0.0)
    inner = jnp.einsum('bhid,bhjd->bhij', q, k) * decay[None]
    xi = jnp.exp(log_gamma[:, None] * (idx + 1.0))
    o = (jnp.einsum('bhij,bhjv->bhiv', inner, v)
         + jnp.einsum('bhid,bhdv->bhiv', q, s) * xi[None, :, :, None])
    zeta = jnp.exp(log_gamma[:, None] * (C - 1.0 - idx))
    s_new = (s * jnp.exp(log_gamma * C)[None, :, None, None]
             + jnp.einsum('bhjd,bhjv->bhdv', k * zeta[None, :, :, None], v))
    return o, s_new


def retention_seq(q, k, v, s0):
    B, H, T, dk = q.shape
    C = RET_CHUNK if T % RET_CHUNK == 0 else T
    n = T // C
    log_gamma = ret_log_gamma()

    def to_chunks(a):
        return jnp.moveaxis(a.reshape(B, H, n, C, a.shape[-1]), 2, 0)

    def step(s, qkv):
        qc, kc, vc = qkv
        o, s_new = retention_chunk(qc, kc, vc, s, log_gamma)
        return s_new, o

    s_fin, o = lax.scan(step, s0, (to_chunks(q), to_chunks(k), to_chunks(v)))
    o = jnp.moveaxis(o, 0, 2).reshape(B, H, T, -1)
    return o, s_fin


def retention_mixer(q, k, v, gate, pos, s0, ret_norm_g):
    B, T, _ = q.shape

    def heads(a):
        return a.reshape(B, T, RET_HEADS, RET_HEAD_DIM).transpose(0, 2, 1, 3).astype(F32)

    qh = rotary(heads(q), pos)
    kh = rotary(heads(k), pos) * (RET_HEAD_DIM ** -0.5)
    vh = heads(v)
    o, s_new = retention_seq(qh, kh, vh, s0.astype(F32))
    mu = jnp.mean(o, axis=-1, keepdims=True)
    var = jnp.mean(jnp.square(o - mu), axis=-1, keepdims=True)
    o = (o - mu) * lax.rsqrt(var + NORM_EPS)
    o = o.transpose(0, 2, 1, 3).reshape(B, T, RET_WIDTH) * ret_norm_g.astype(F32)
    return jax.nn.silu(gate.astype(F32)) * o, s_new


def mem_kv(mem, g_src, w_mk, w_mv):
    B, M, _ = mem.shape
    m = rmsnorm(mem, g_src)
    k = (m @ w_mk).reshape(B, M, MEM_HEADS, MEM_HEAD_DIM)
    v = (m @ w_mv).reshape(B, M, MEM_HEADS, MEM_HEAD_DIM)
    return k, v


def mem_attn(h, mk, mv, w_mq, w_mo):
    B, T, _ = h.shape
    q = (h @ w_mq).reshape(B, T, MEM_HEADS, MEM_HEAD_DIM)
    s = jnp.einsum('bthd,bmhd->bhtm', q.astype(F32), mk.astype(F32)) * (MEM_HEAD_DIM ** -0.5)
    p = jax.nn.softmax(s, axis=-1)
    o = jnp.einsum('bhtm,bmhd->bthd', p, mv.astype(F32)).reshape(B, T, D_MODEL)
    return o.astype(h.dtype) @ w_mo


def peer(h, peer_wq, peer_subkeys, peer_u, peer_v):
    B, T, D = h.shape
    n = B * T
    xt = h.reshape(n, D)
    pad = (-n) % PEER_BLOCK
    blocks = jnp.pad(xt, ((0, pad), (0, 0))).reshape(-1, PEER_BLOCK, D)
    subkeys = peer_subkeys.astype(F32)

    def one(xb):
        q = (xb @ peer_wq).reshape(PEER_BLOCK, PEER_HEADS, 2, PEER_HALF).astype(F32)
        s = jnp.einsum('thcd,hckd->thck', q, subkeys)
        sv, si = lax.top_k(s, PEER_TOPK)
        cand = sv[:, :, 0, :, None] + sv[:, :, 1, None, :]
        cidx = si[:, :, 0, :, None] * PEER_N_KEYS + si[:, :, 1, None, :]
        cand = cand.reshape(PEER_BLOCK, PEER_HEADS, PEER_TOPK * PEER_TOPK)
        cidx = cidx.reshape(PEER_BLOCK, PEER_HEADS, PEER_TOPK * PEER_TOPK)
        top_s, sel = lax.top_k(cand, PEER_TOPK)
        expert = jnp.take_along_axis(cidx, sel, axis=-1)
        g = jax.nn.softmax(top_s, axis=-1)
        u = jnp.take(peer_u, expert, axis=0).astype(F32)
        act = jax.nn.gelu(jnp.einsum('thkd,td->thk', u, xb.astype(F32)), approximate=False)
        vv = jnp.take(peer_v, expert, axis=0).astype(F32)
        return jnp.einsum('thk,thkd->td', g * act, vv).astype(h.dtype)

    out = lax.map(one, blocks).reshape(-1, D)[:n]
    return out.reshape(B, T, D)


def layer(x, pos, pool_prefix, s0, mk, mv, norm_mix_g, w_in, pool_w, pool_scale,
          ret_norm_g, w_out, norm_mem_g, w_mq, w_mo, norm_ffn_g, peer_wq,
          peer_subkeys, peer_u, peer_v):
    h = rmsnorm(x, norm_mix_g)
    proj = h @ w_in
    p = proj[..., :POOL_WIDTH]
    q, k, v, gate = jnp.split(proj[..., POOL_WIDTH:], 4, axis=-1)
    p_ext = jnp.concatenate([pool_prefix.astype(p.dtype), p], axis=1)
    a = pool_mix(p_ext, pos, pool_w, pool_scale)
    new_pool = p_ext[:, -POOL_BUF:]
    b, s_new = retention_mixer(q, k, v, gate, pos, s0, ret_norm_g)
    x = x + jnp.concatenate([a, b], axis=-1).astype(x.dtype) @ w_out
    x = x + mem_attn(rmsnorm(x, norm_mem_g), mk, mv, w_mq, w_mo)
    x = x + peer(rmsnorm(x, norm_ffn_g), peer_wq, peer_subkeys, peer_u, peer_v)
    return x, new_pool, s_new


def setup_inputs(seed: int = 0) -> dict:
    key = jax.random.key(seed)
    ks = jax.random.split(key, 32)
    nrm = lambda k, shape, s: jax.random.normal(k, shape, F32) * s
    gain = lambda k, shape: 1.0 + 0.02 * jax.random.normal(k, shape, F32)
    L = DEPTH
    D = D_MODEL
    return {
        "x_prompt": nrm(ks[0], (BATCH, SEQ, D), 1.0),
        "x_sample": nrm(ks[1], (DEC_BATCH, DEC_SEQ, D), 1.0),
        "state_pool": nrm(ks[2], (L, DEC_BATCH, POOL_BUF, POOL_WIDTH), 1.0),
        "state_ret": nrm(ks[3], (L, DEC_BATCH, RET_HEADS, RET_HEAD_DIM, RET_HEAD_DIM), 0.5),
        "cache_mem_k": nrm(ks[4], (L, DEC_BATCH, MEM_TOKENS, MEM_HEADS, MEM_HEAD_DIM), 1.0),
        "cache_mem_v": nrm(ks[5], (L, DEC_BATCH, MEM_TOKENS, MEM_HEADS, MEM_HEAD_DIM), 1.0),
        "mem_prompt": nrm(ks[6], (BATCH, MEM_TOKENS, D), 1.0),
        "norm_mix_g": gain(ks[7], (L, D)),
        "w_in": nrm(ks[8], (L, D, IN_PROJ_WIDTH), D ** -0.5),
        "pool_w": nrm(ks[9], (L, POOL_GROUPS, POOL_GROUP_DIM, POOL_GROUP_DIM), POOL_GROUP_DIM ** -0.5),
        "pool_scale": gain(ks[10], (L, POOL_WIDTH)),
        "ret_norm_g": gain(ks[11], (L, RET_WIDTH)),
        "w_out": nrm(ks[12], (L, D, D), D ** -0.5),
        "norm_mem_g": gain(ks[13], (L, D)),
        "norm_memsrc_g": gain(ks[14], (L, D)),
        "w_mq": nrm(ks[15], (L, D, D), D ** -0.5),
        "w_mk": nrm(ks[16], (L, D, D), D ** -0.5),
        "w_mv": nrm(ks[17], (L, D, D), D ** -0.5),
        "w_mo": nrm(ks[18], (L, D, D), D ** -0.5),
        "norm_ffn_g": gain(ks[19], (L, D)),
        "peer_wq": nrm(ks[20], (L, D, PEER_HEADS * PEER_QUERY_DIM), D ** -0.5),
        "peer_subkeys": nrm(ks[21], (L, PEER_HEADS, 2, PEER_N_KEYS, PEER_HALF), PEER_HALF ** -0.5),
        "peer_u": nrm(ks[22], (L, PEER_N_EXPERTS, D), D ** -0.5),
        "peer_v": nrm(ks[23], (L, PEER_N_EXPERTS, D), PEER_HEADS ** -0.5),
        "norm_final_g": gain(ks[24], (D,)),
    }


def reference(x_prompt, x_sample, state_pool, state_ret, cache_mem_k, cache_mem_v, mem_prompt,
              norm_mix_g, w_in, pool_w, pool_scale, ret_norm_g, w_out, norm_mem_g,
              norm_memsrc_g, w_mq, w_mk, w_mv, w_mo, norm_ffn_g, peer_wq, peer_subkeys,
              peer_u, peer_v, norm_final_g):
    Bp, Tp, _ = x_prompt.shape
    Bs, Ts, _ = x_sample.shape
    pos_p = jnp.arange(Tp, dtype=jnp.int32)
    pos_s = PAST_LEN + jnp.arange(Ts, dtype=jnp.int32)
    pool0 = jnp.zeros((Bp, POOL_BUF, POOL_WIDTH), x_prompt.dtype)
    ret0 = jnp.zeros((Bp, RET_HEADS, RET_HEAD_DIM, RET_HEAD_DIM), F32)

    yp, ys = x_prompt, x_sample
    pool_p_l, ret_p_l, mk_p_l, mv_p_l, pool_s_l, ret_s_l = [], [], [], [], [], []
    for l in range(DEPTH):
        w = (norm_mix_g[l], w_in[l], pool_w[l], pool_scale[l], ret_norm_g[l], w_out[l],
             norm_mem_g[l], w_mq[l], w_mo[l], norm_ffn_g[l], peer_wq[l], peer_subkeys[l],
             peer_u[l], peer_v[l])
        mk_p, mv_p = mem_kv(mem_prompt, norm_memsrc_g[l], w_mk[l], w_mv[l])
        yp, pool_p, ret_p = layer(yp, pos_p, pool0, ret0, mk_p, mv_p, *w)
        ys, pool_s, ret_s = layer(ys, pos_s, state_pool[l], state_ret[l],
                                  cache_mem_k[l], cache_mem_v[l], *w)
        pool_p_l.append(pool_p)
        ret_p_l.append(ret_p.astype(x_prompt.dtype))
        mk_p_l.append(mk_p)
        mv_p_l.append(mv_p)
        pool_s_l.append(pool_s.astype(state_pool.dtype))
        ret_s_l.append(ret_s.astype(state_ret.dtype))

    y_prompt = rmsnorm(yp, norm_final_g)
    y_sample = rmsnorm(ys, norm_final_g)
    return (y_prompt, y_sample, jnp.stack(pool_p_l), jnp.stack(ret_p_l), jnp.stack(mk_p_l),
            jnp.stack(mv_p_l), jnp.stack(pool_s_l), jnp.stack(ret_s_l))
```

```python
import functools

import jax
import jax.numpy as jnp
from jax import lax
from jax.experimental import pallas as pl
from jax.experimental.pallas import tpu as pltpu

F32 = jnp.float32
BF16 = jnp.bfloat16

NORM_EPS = 1e-6
PAST_LEN = 16384
ROPE_BASE = 10000.0
POOL_WINDOWS = (2, 4, 8, 16)
POOL_BUF = max(POOL_WINDOWS) - 1
POOL_PAD = POOL_BUF + 1
RET_HEADS = 4
MEM_HEADS = 4
PEER_HEADS = 8
PEER_TOPK = 16

LANES = 128
SUBLANES = 8
VMEM_LIMIT = 52 * 1024 * 1024
NEG_INF = float("-inf")


def _params(*sem):
    return pltpu.CompilerParams(dimension_semantics=sem, vmem_limit_bytes=VMEM_LIMIT)


def _pick(n, pref, mult=SUBLANES):
    for c in range(min(pref, n), 0, -1):
        if n % c == 0 and c % mult == 0:
            return c
    raise ValueError(f"no tile for {n} (pref {pref}, mult {mult})")


def _dot(a, b):
    return jnp.dot(a, b, preferred_element_type=F32)


def _dot_nt(a, b):
    return lax.dot_general(a, b, (((1,), (1,)), ((), ())), preferred_element_type=F32)


def _dot_tn(a, b):
    return lax.dot_general(a, b, (((0,), (0,)), ((), ())), preferred_element_type=F32)


def _rms(x, g):
    ms = jnp.mean(x * x, axis=-1, keepdims=True)
    return x * lax.rsqrt(ms + NORM_EPS) * g


def _norm_mm_kernel(x_ref, g_ref, w_ref, o_ref, *rest, emit_h):
    if emit_h:
        h_out_ref, h_ref = rest
    else:
        (h_ref,) = rest

    @pl.when(pl.program_id(1) == 0)
    def _():
        h = _rms(x_ref[...], g_ref[...]).astype(BF16)
        h_ref[...] = h
        if emit_h:
            h_out_ref[...] = h

    o_ref[...] = _dot(h_ref[...], w_ref[...]).astype(o_ref.dtype)


def norm_matmul(x, g, w, *, out_dtype=F32, emit_h=False, tm=1024, tn=512):
    M, K = x.shape
    N = w.shape[1]
    tm, tn = _pick(M, tm), _pick(N, tn, LANES)
    out_shape = [jax.ShapeDtypeStruct((M, N), out_dtype)]
    out_specs = [pl.BlockSpec((tm, tn), lambda i, j: (i, j))]
    if emit_h:
        out_shape.append(jax.ShapeDtypeStruct((M, K), BF16))
        out_specs.append(pl.BlockSpec((tm, K), lambda i, j: (i, 0)))
    res = pl.pallas_call(
        functools.partial(_norm_mm_kernel, emit_h=emit_h),
        grid=(M // tm, N // tn),
        in_specs=[pl.BlockSpec((tm, K), lambda i, j: (i, 0)),
                  pl.BlockSpec((1, K), lambda i, j: (0, 0)),
                  pl.BlockSpec((K, tn), lambda i, j: (0, j))],
        out_specs=out_specs,
        out_shape=out_shape,
        scratch_shapes=[pltpu.VMEM((tm, K), BF16)],
        compiler_params=_params("parallel", "arbitrary"),
        name="norm_matmul",
    )(x, g.reshape(1, K), w)
    return res if emit_h else res[0]


def _res_mm_kernel(*refs, n_lhs):
    res_ref = refs[0]
    a_refs = refs[1:1 + n_lhs]
    w_refs = refs[1 + n_lhs:1 + 2 * n_lhs]
    o_ref = refs[1 + 2 * n_lhs]
    acc = res_ref[...]
    for a_ref, w_ref in zip(a_refs, w_refs):
        acc = acc + _dot(a_ref[...], w_ref[...])
    o_ref[...] = acc


def residual_matmul(res, lhs, ws, *, tm=1024, tn=512):
    M, N = res.shape
    tm, tn = _pick(M, tm), _pick(N, tn, LANES)
    n = len(lhs)
    in_specs = [pl.BlockSpec((tm, tn), lambda i, j: (i, j))]
    in_specs += [pl.BlockSpec((tm, a.shape[1]), lambda i, j: (i, 0)) for a in lhs]
    in_specs += [pl.BlockSpec((w.shape[0], tn), lambda i, j: (0, j)) for w in ws]
    return pl.pallas_call(
        functools.partial(_res_mm_kernel, n_lhs=n),
        grid=(M // tm, N // tn),
        in_specs=in_specs,
        out_specs=pl.BlockSpec((tm, tn), lambda i, j: (i, j)),
        out_shape=jax.ShapeDtypeStruct((M, N), F32),
        compiler_params=_params("parallel", "arbitrary"),
        name="residual_matmul",
    )(res, *lhs, *ws)


def _pool_prompt_kernel(p_ref, pw_ref, sc_ref, a_ref, ext_ref, *, rows):
    T, GD = p_ref.shape
    g = pl.program_id(1)
    ext_ref[0:POOL_PAD, :] = jnp.zeros((POOL_PAD, GD), F32)
    ext_ref[POOL_PAD:, :] = p_ref[...]
    for gi, w in enumerate(POOL_WINDOWS):
        @pl.when(g == gi)
        def _(w=w):
            for r0 in range(0, T, rows):
                acc = ext_ref[POOL_PAD + r0:POOL_PAD + r0 + rows, :]
                tok = acc
                for s in range(1, w):
                    acc = acc + ext_ref[POOL_PAD + r0 - s:POOL_PAD + r0 - s + rows, :]
                t = lax.broadcasted_iota(jnp.int32, (rows, GD), 0) + r0
                cnt = jnp.minimum(t + 1, w).astype(F32)
                d = acc / cnt - tok
                y = _dot(d.astype(BF16), pw_ref[0]) * sc_ref[...]
                a_ref[r0:r0 + rows, :] = y.astype(BF16)


def pool_prompt(proj, pool_w, pool_scale, n_seq, T):
    G, GD, _ = pool_w.shape
    W = G * GD
    return pl.pallas_call(
        functools.partial(_pool_prompt_kernel, rows=_pick(T, 256)),
        grid=(n_seq, G),
        in_specs=[pl.BlockSpec((T, GD), lambda b, g: (b, g)),
                  pl.BlockSpec((1, GD, GD), lambda b, g: (g, 0, 0)),
                  pl.BlockSpec((1, GD), lambda b, g: (0, g))],
        out_specs=pl.BlockSpec((T, GD), lambda b, g: (b, g)),
        out_shape=jax.ShapeDtypeStruct((n_seq * T, W), BF16),
        scratch_shapes=[pltpu.VMEM((POOL_PAD + T, GD), F32)],
        compiler_params=_params("parallel", "arbitrary"),
        name="pool_prompt",
    )(proj, pool_w, pool_scale.reshape(1, W))


def _pool_sample_kernel(pre_ref, p_ref, pw_ref, sc_ref, a_ref):
    Ts = p_ref.shape[0]
    g = pl.program_id(0)

    def ext(r):
        return pre_ref[r] if r < POOL_BUF else p_ref[r - POOL_BUF]

    for gi, w in enumerate(POOL_WINDOWS):
        @pl.when(g == gi)
        def _(w=w):
            for t in range(Ts):
                acc = ext(POOL_BUF + t)
                tok = acc
                for s in range(1, w):
                    acc = acc + ext(POOL_BUF + t - s)
                cnt = float(min(PAST_LEN + t + 1, w))
                d = acc / cnt - tok
                y = _dot(d.astype(BF16), pw_ref[0]) * sc_ref[...]
                a_ref[t] = y.astype(BF16)


def pool_sample(pre_t, p_t, pool_w, pool_scale):
    G, GD, _ = pool_w.shape
    W = G * GD
    Ts, B, _ = p_t.shape
    return pl.pallas_call(
        _pool_sample_kernel,
        grid=(G,),
        in_specs=[pl.BlockSpec((POOL_BUF, B, GD), lambda g: (0, 0, g)),
                  pl.BlockSpec((Ts, B, GD), lambda g: (0, 0, g)),
                  pl.BlockSpec((1, GD, GD), lambda g: (g, 0, 0)),
                  pl.BlockSpec((1, GD), lambda g: (0, g))],
        out_specs=pl.BlockSpec((Ts, B, GD), lambda g: (0, 0, g)),
        out_shape=jax.ShapeDtypeStruct((Ts, B, W), BF16),
        compiler_params=_params("arbitrary"),
        name="pool_sample",
    )(pre_t, p_t, pool_w, pool_scale.reshape(1, W))


def _rotary(x, cos, sin):
    half = x.shape[-1] // 2
    x1, x2 = x[:, :half], x[:, half:]
    return jnp.concatenate([x1 * cos - x2 * sin, x1 * sin + x2 * cos], axis=-1)


def _retention_chunk(q, k, v, gate, cos, sin, gn, s, lg):
    C, dk = q.shape
    qr = _rotary(q, cos, sin).astype(BF16)
    kr = _rotary(k, cos, sin) * (dk ** -0.5)
    vb = v.astype(BF16)
    ri = lax.broadcasted_iota(jnp.int32, (C, C), 0)
    ci = lax.broadcasted_iota(jnp.int32, (C, C), 1)
    diff = (ri - ci).astype(F32)
    decay = jnp.where(diff >= 0, jnp.exp(lg * jnp.maximum(diff, 0.0)), 0.0)
    inner = _dot_nt(qr, kr.astype(BF16)) * decay
    idx = lax.broadcasted_iota(jnp.int32, (C, dk), 0).astype(F32)
    xi = jnp.exp(lg * (idx + 1.0))
    zeta = jnp.exp(lg * (C - 1.0 - idx))
    o = _dot(inner.astype(BF16), vb) + _dot(qr, s.astype(BF16)) * xi
    g_c = jnp.exp(jnp.full((1, s.shape[1]), C, F32) * lg)
    s_new = s * g_c + _dot_tn((kr * zeta).astype(BF16), vb)
    mu = jnp.mean(o, axis=-1, keepdims=True)
    oc = o - mu
    var = jnp.mean(oc * oc, axis=-1, keepdims=True)
    on = oc * lax.rsqrt(var + NORM_EPS) * gn
    return gate * jax.nn.sigmoid(gate) * on, s_new


def _ret_prompt_kernel(lg_ref, q_ref, k_ref, v_ref, gate_ref, cos_ref, sin_ref, gn_ref,
                       b_ref, s_out_ref, s_ref):
    h = pl.program_id(1)
    c = pl.program_id(2)

    @pl.when(c == 0)
    def _():
        s_ref[...] = jnp.zeros(s_ref.shape, F32)

    out, s_new = _retention_chunk(q_ref[...], k_ref[...], v_ref[...], gate_ref[...],
                                  cos_ref[...], sin_ref[...], gn_ref[...], s_ref[...],
                                  lg_ref[h])
    b_ref[...] = out.astype(BF16)
    s_ref[...] = s_new

    @pl.when(c == pl.num_programs(2) - 1)
    def _():
        s_out_ref[0, 0] = s_new


def retention_prompt(proj, cos, sin, ret_norm_g, log_gamma, n_seq, T, width):
    H = RET_HEADS
    dk = width // H
    C = _pick(T, 256)
    nC = T // C

    def col(part):
        return lambda b, h, c: (b * nC + c, part * H + h)

    return pl.pallas_call(
        _ret_prompt_kernel,
        grid=(n_seq, H, nC),
        in_specs=[pl.BlockSpec(memory_space=pltpu.SMEM),
                  pl.BlockSpec((C, dk), col(1)),
                  pl.BlockSpec((C, dk), col(2)),
                  pl.BlockSpec((C, dk), col(3)),
                  pl.BlockSpec((C, dk), col(4)),
                  pl.BlockSpec((C, dk // 2), lambda b, h, c: (c, 0)),
                  pl.BlockSpec((C, dk // 2), lambda b, h, c: (c, 0)),
                  pl.BlockSpec((1, dk), lambda b, h, c: (0, h))],
        out_specs=[pl.BlockSpec((C, dk), lambda b, h, c: (b * nC + c, h)),
                   pl.BlockSpec((1, 1, dk, dk), lambda b, h, c: (b, h, 0, 0))],
        out_shape=[jax.ShapeDtypeStruct((n_seq * T, width), BF16),
                   jax.ShapeDtypeStruct((n_seq, H, dk, dk), F32)],
        scratch_shapes=[pltpu.VMEM((dk, dk), F32)],
        compiler_params=_params("parallel", "parallel", "arbitrary"),
        name="retention_prompt",
    )(log_gamma, proj, proj, proj, proj, cos, sin, ret_norm_g.reshape(1, width))


def _ret_sample_kernel(lg_ref, q_ref, k_ref, v_ref, gate_ref, cos_ref, sin_ref, gn_ref,
                       s_in_ref, b_ref, s_out_ref, *, n_seq, Ts):
    H = RET_HEADS
    dk = q_ref.shape[1] // H
    cos, sin = cos_ref[...], sin_ref[...]
    for b in range(n_seq):
        rows = slice(b * Ts, (b + 1) * Ts)
        for h in range(H):
            cols = slice(h * dk, (h + 1) * dk)
            out, s_new = _retention_chunk(q_ref[rows, cols], k_ref[rows, cols],
                                          v_ref[rows, cols], gate_ref[rows, cols],
                                          cos, sin, gn_ref[:, cols], s_in_ref[b, h],
                                          lg_ref[h])
            b_ref[rows, cols] = out
            s_out_ref[b, h] = s_new


def retention_sample(proj, state, cos, sin, ret_norm_g, log_gamma, row0, n_seq, Ts, width):
    H = RET_HEADS
    dk = width // H
    bb = _pick(n_seq, 4, 1)
    rows = bb * Ts
    blk0 = row0 // rows

    def col(part):
        return lambda i: (blk0 + i, part)

    return pl.pallas_call(
        functools.partial(_ret_sample_kernel, n_seq=bb, Ts=Ts),
        grid=(n_seq // bb,),
        in_specs=[pl.BlockSpec(memory_space=pltpu.SMEM),
                  pl.BlockSpec((rows, width), col(1)),
                  pl.BlockSpec((rows, width), col(2)),
                  pl.BlockSpec((rows, width), col(3)),
                  pl.BlockSpec((rows, width), col(4)),
                  pl.BlockSpec((Ts, dk // 2), lambda i: (0, 0)),
                  pl.BlockSpec((Ts, dk // 2), lambda i: (0, 0)),
                  pl.BlockSpec((1, width), lambda i: (0, 0)),
                  pl.BlockSpec((bb, H, dk, dk), lambda i: (i, 0, 0, 0))],
        out_specs=[pl.BlockSpec((rows, width), lambda i: (i, 0)),
                   pl.BlockSpec((bb, H, dk, dk), lambda i: (i, 0, 0, 0))],
        out_shape=[jax.ShapeDtypeStruct((n_seq * Ts, width), F32),
                   jax.ShapeDtypeStruct((n_seq, H, dk, dk), F32)],
        compiler_params=_params("parallel"),
        name="retention_sample",
    )(log_gamma, proj, proj, proj, proj, cos, sin, ret_norm_g.reshape(1, width), state)


def _attend(q, k, v):
    s = _dot_nt(q.astype(BF16), k.astype(BF16)) * (q.shape[-1] ** -0.5)
    e = jnp.exp(s - jnp.max(s, axis=-1, keepdims=True))
    p = e / jnp.sum(e, axis=-1, keepdims=True)
    return _dot(p.astype(BF16), v.astype(BF16))


def _attn_prompt_kernel(q_ref, k_ref, v_ref, o_ref):
    o_ref[...] = _attend(q_ref[...], k_ref[...], v_ref[...]).astype(BF16)


def mem_attn_prompt(q, mk, mv, n_seq, T):
    D = mk.shape[1]
    M = mk.shape[0] // n_seq
    hd = D // MEM_HEADS
    tq = _pick(T, 512)
    nq = T // tq
    return pl.pallas_call(
        _attn_prompt_kernel,
        grid=(n_seq, nq, MEM_HEADS),
        in_specs=[pl.BlockSpec((tq, hd), lambda b, i, h: (b * nq + i, h)),
                  pl.BlockSpec((M, hd), lambda b, i, h: (b, h)),
                  pl.BlockSpec((M, hd), lambda b, i, h: (b, h))],
        out_specs=pl.BlockSpec((tq, hd), lambda b, i, h: (b * nq + i, h)),
        out_shape=jax.ShapeDtypeStruct((n_seq * T, D), BF16),
        compiler_params=_params("parallel", "parallel", "arbitrary"),
        name="mem_attn_prompt",
    )(q, mk, mv)


def _attn_sample_kernel(q_ref, k_ref, v_ref, o_ref, *, n_seq, Ts):
    hd = q_ref.shape[1] // MEM_HEADS
    for b in range(n_seq):
        rows = slice(b * Ts, (b + 1) * Ts)
        for h in range(MEM_HEADS):
            cols = slice(h * hd, (h + 1) * hd)
            o_ref[rows, cols] = _attend(q_ref[rows, cols], k_ref[b, :, cols],
                                        v_ref[b, :, cols])


def mem_attn_sample(q, ck, cv, row0, n_seq, Ts):
    _, M, D = ck.shape
    bb = _pick(n_seq, 2, 1)
    rows = bb * Ts
    blk0 = row0 // rows
    return pl.pallas_call(
        functools.partial(_attn_sample_kernel, n_seq=bb, Ts=Ts),
        grid=(n_seq // bb,),
        in_specs=[pl.BlockSpec((rows, D), lambda i: (blk0 + i, 0)),
                  pl.BlockSpec((bb, M, D), lambda i: (i, 0, 0)),
                  pl.BlockSpec((bb, M, D), lambda i: (i, 0, 0))],
        out_specs=pl.BlockSpec((rows, D), lambda i: (i, 0)),
        out_shape=jax.ShapeDtypeStruct((n_seq * Ts, D), F32),
        compiler_params=_params("parallel"),
        name="mem_attn_sample",
    )(q, ck, cv)


def _top16(s, vals_ref):
    kio = lax.broadcasted_iota(jnp.int32, s.shape, 0).astype(F32)
    rank = jnp.full(s.shape, float(PEER_TOPK), F32)
    cur = s
    for a in range(PEER_TOPK):
        m = jnp.max(cur, axis=0, keepdims=True)
        first = jnp.min(jnp.where(cur == m, kio, float(s.shape[0])), axis=0, keepdims=True)
        hit = kio == first
        rank = jnp.where(hit, float(a), rank)
        cur = jnp.where(hit, NEG_INF, cur)
        vals_ref[a:a + 1, :] = m
    return rank


def _candidate_grid(v1_ref, v2_ref):
    K = PEER_TOPK
    L = v1_ref.shape[1]
    v2_lo = v2_ref[0:SUBLANES, :]
    bio = lax.broadcasted_iota(jnp.int32, (SUBLANES, L), 0)
    bio_f = bio.astype(F32)
    cands = [v1_ref[0:1, :] + v2_ref[...]]
    cidx = [lax.broadcasted_iota(jnp.int32, (K, L), 0).astype(F32)]
    pieces = [(0, K, 0)]
    row = K
    for a in range(1, SUBLANES):
        limit = K // (a + 1)
        c = v1_ref[a:a + 1, :] + v2_lo
        if limit < SUBLANES:
            c = jnp.where(bio < limit, c, NEG_INF)
        cands.append(c)
        cidx.append(bio_f + float(a * K))
        pieces.append((row, SUBLANES, a))
        row += SUBLANES
    cands.append(v1_ref[SUBLANES:K, :] + v2_ref[0:1, :])
    cidx.append((bio_f + float(SUBLANES)) * float(K))
    pieces.append((row, SUBLANES, None))
    return jnp.concatenate(cands, axis=0), jnp.concatenate(cidx, axis=0), pieces


def _peer_route_kernel(q_ref, sk_ref, e1_ref, n_ref, r2_ref, e2_ref, v1_ref, v2_ref):
    T = q_ref.shape[0]
    half = sk_ref.shape[3]
    K = PEER_TOPK
    for t0 in range(0, T, LANES):
        lanes = slice(t0, t0 + LANES)
        q = q_ref[lanes, :].astype(BF16)
        s1 = _dot_nt(sk_ref[0, 0], q[:, :half])
        s2 = _dot_nt(sk_ref[0, 1], q[:, half:])
        r1 = _top16(s1, v1_ref)
        r2 = _top16(s2, v2_ref)

        cand, cidx, pieces = _candidate_grid(v1_ref, v2_ref)
        cur = cand
        sel = jnp.zeros(cand.shape, F32)
        for _ in range(K):
            m = jnp.max(cur, axis=0, keepdims=True)
            first = jnp.min(jnp.where(cur == m, cidx, float(K * K)), axis=0, keepdims=True)
            hit = cidx == first
            sel = jnp.where(hit, 1.0, sel)
            cur = jnp.where(hit, NEG_INF, cur)

        top = v1_ref[0:1, :] + v2_ref[0:1, :]
        z = jnp.sum(sel * jnp.exp(cand - top), axis=0, keepdims=True)
        n = jnp.zeros(s1.shape, F32)
        for row0, rows, a in pieces:
            piece = sel[row0:row0 + rows, :]
            if a is None:
                for r in range(rows):
                    n = jnp.where(r1 == float(SUBLANES + r), piece[r:r + 1, :], n)
            else:
                n = jnp.where(r1 == float(a), jnp.sum(piece, axis=0, keepdims=True), n)

        e1 = jnp.where(r1 < float(K), jnp.exp(s1 - v1_ref[0:1, :]), 0.0) / z
        e1_ref[0, :, lanes] = e1
        n_ref[0, :, lanes] = n
        r2_ref[0, :, lanes] = r2
        e2_ref[0, :, lanes] = jnp.exp(s2 - v2_ref[0:1, :])


def peer_route(qp, subkeys, tile):
    n_tok = qp.shape[0]
    heads, _, n_keys, half = subkeys.shape
    spec = pl.BlockSpec((1, n_keys, tile), lambda i, h: (h, 0, i))
    shape = jax.ShapeDtypeStruct((heads, n_keys, n_tok), F32)
    return pl.pallas_call(
        _peer_route_kernel,
        grid=(n_tok // tile, heads),
        in_specs=[pl.BlockSpec((tile, 2 * half), lambda i, h: (i, h)),
                  pl.BlockSpec((1, 2, n_keys, half), lambda i, h: (h, 0, 0, 0))],
        out_specs=[spec] * 4,
        out_shape=[shape] * 4,
        scratch_shapes=[pltpu.VMEM((PEER_TOPK, LANES), F32)] * 2,
        compiler_params=_params("parallel", "arbitrary"),
        name="peer_route",
    )(qp, subkeys)


def _peer_dense_kernel(xt_ref, u_ref, vt_ref, e1_ref, n_ref, r2_ref, e2_ref, o_ref,
                       h_ref, a_ref):
    eb = pl.program_id(1)
    heads, n_keys, T = r2_ref.shape
    per_blk = e1_ref.shape[1]

    @pl.when(eb == 0)
    def _():
        o_ref[...] = jnp.zeros(o_ref.shape, F32)

    h_ref[...] = _dot(u_ref[...], xt_ref[...])
    for ii in range(per_blk):
        rows = slice(ii * n_keys, (ii + 1) * n_keys)
        for t0 in range(0, T, LANES):
            lanes = slice(t0, t0 + LANES)
            g = jnp.zeros((n_keys, LANES), F32)
            for h in range(heads):
                n = n_ref[h, ii:ii + 1, lanes]
                e1 = e1_ref[h, ii:ii + 1, lanes]
                g = g + jnp.where(r2_ref[h, :, lanes] < n, e2_ref[h, :, lanes], 0.0) * e1
            pre = h_ref[rows, lanes]
            act = 0.5 * pre * (1.0 + lax.erf(pre * (2.0 ** -0.5)))
            a_ref[rows, lanes] = (g * act).astype(BF16)
    o_ref[...] += _dot(vt_ref[...], a_ref[...])


def peer_dense(xt, u, vt, tables, tile, e_blk):
    D, n_tok = xt.shape
    n_exp = u.shape[0]
    heads, n_keys, _ = tables[0].shape
    assert e_blk % (SUBLANES * n_keys) == 0
    key_spec = pl.BlockSpec((heads, n_keys, tile), lambda i, e: (0, 0, i))
    blk_spec = pl.BlockSpec((heads, e_blk // n_keys, tile), lambda i, e: (0, e, i))
    return pl.pallas_call(
        _peer_dense_kernel,
        grid=(n_tok // tile, n_exp // e_blk),
        in_specs=[pl.BlockSpec((D, tile), lambda i, e: (0, i)),
                  pl.BlockSpec((e_blk, D), lambda i, e: (e, 0)),
                  pl.BlockSpec((D, e_blk), lambda i, e: (0, e)),
                  blk_spec, blk_spec, key_spec, key_spec],
        out_specs=pl.BlockSpec((D, tile), lambda i, e: (0, i)),
        out_shape=jax.ShapeDtypeStruct((D, n_tok), F32),
        scratch_shapes=[pltpu.VMEM((e_blk, tile), F32), pltpu.VMEM((e_blk, tile), BF16)],
        compiler_params=_params("parallel", "arbitrary"),
        name="peer_dense",
    )(xt, u, vt, *tables)


def _final_kernel(x_ref, p_ref, g_ref, o_ref):
    o_ref[...] = _rms(x_ref[...] + p_ref[...], g_ref[...])


def final_norm(x, p, g, tm=512):
    M, D = x.shape
    tm = _pick(M, tm)
    spec = pl.BlockSpec((tm, D), lambda i: (i, 0))
    return pl.pallas_call(
        _final_kernel,
        grid=(M // tm,),
        in_specs=[spec, spec, pl.BlockSpec((1, D), lambda i: (0, 0))],
        out_specs=spec,
        out_shape=jax.ShapeDtypeStruct((M, D), F32),
        compiler_params=_params("parallel"),
        name="final_norm",
    )(x, p, g.reshape(1, D))


def _rope_tables(pos, half):
    inv = 1.0 / (ROPE_BASE ** (jnp.arange(half, dtype=F32) / half))
    ang = pos.astype(F32)[:, None] * inv[None, :]
    return jnp.cos(ang), jnp.sin(ang)


def kernel(x_prompt, x_sample, state_pool, state_ret, cache_mem_k, cache_mem_v, mem_prompt,
           norm_mix_g, w_in, pool_w, pool_scale, ret_norm_g, w_out, norm_mem_g,
           norm_memsrc_g, w_mq, w_mk, w_mv, w_mo, norm_ffn_g, peer_wq, peer_subkeys,
           peer_u, peer_v, norm_final_g):
    Bp, Tp, D = x_prompt.shape
    Bs, Ts, _ = x_sample.shape
    depth = w_in.shape[0]
    assert depth == 1, "single-layer step"
    n_p, n_s = Bp * Tp, Bs * Ts
    n_tok = n_p + n_s
    W = pool_w.shape[1] * pool_w.shape[2]
    dk = W // RET_HEADS
    M = mem_prompt.shape[1]
    n_keys = peer_subkeys.shape[3]
    assert POOL_BUF >= Ts and w_in.shape[2] == 5 * W

    bf = lambda a: a.astype(BF16)
    x0 = jnp.concatenate([x_prompt.reshape(n_p, D), x_sample.reshape(n_s, D)], axis=0)

    proj = norm_matmul(x0, norm_mix_g[0], bf(w_in[0]))
    a_p = pool_prompt(proj, bf(pool_w[0]), pool_scale[0], Bp, Tp)
    p_s = proj[n_p:, :W].reshape(Bs, Ts, W)
    a_s = pool_sample(state_pool[0].transpose(1, 0, 2), p_s.transpose(1, 0, 2),
                      bf(pool_w[0]), pool_scale[0]).transpose(1, 0, 2).reshape(n_s, W)

    log_gamma = jnp.log(1.0 - 2.0 ** (-5.0 - jnp.arange(RET_HEADS, dtype=F32)))
    cos_p, sin_p = _rope_tables(jnp.arange(Tp, dtype=jnp.int32), dk // 2)
    cos_s, sin_s = _rope_tables(PAST_LEN + jnp.arange(Ts, dtype=jnp.int32), dk // 2)
    b_p, ret_p = retention_prompt(proj, cos_p, sin_p, ret_norm_g[0], log_gamma, Bp, Tp, W)
    b_s, ret_s = retention_sample(proj, state_ret[0], cos_s, sin_s, ret_norm_g[0],
                                  log_gamma, n_p, Bs, Ts, W)

    w_o = bf(w_out[0])
    x1 = residual_matmul(x0, [jnp.concatenate([a_p, a_s], axis=0),
                              jnp.concatenate([b_p, bf(b_s)], axis=0)], [w_o[:W], w_o[W:]])

    mem = mem_prompt.reshape(Bp * M, D)
    mk = norm_matmul(mem, norm_memsrc_g[0], bf(w_mk[0]))
    mv = norm_matmul(mem, norm_memsrc_g[0], bf(w_mv[0]))
    qm = norm_matmul(x1, norm_mem_g[0], bf(w_mq[0]))
    o_p = mem_attn_prompt(qm, mk, mv, Bp, Tp)
    o_s = mem_attn_sample(qm, cache_mem_k[0].reshape(Bs, M, D), cache_mem_v[0].reshape(Bs, M, D),
                          n_p, Bs, Ts)
    x2 = residual_matmul(x1, [jnp.concatenate([o_p, bf(o_s)], axis=0)], [bf(w_mo[0])])

    qp, h3 = norm_matmul(x2, norm_ffn_g[0], bf(peer_wq[0]), emit_h=True)
    tile = _pick(n_tok, 512, LANES)
    tables = peer_route(qp, bf(peer_subkeys[0]), tile)
    e_blk = SUBLANES * n_keys
    po_t = peer_dense(h3.T, bf(peer_u[0]), bf(peer_v[0]).T, tables, tile, e_blk)
    y = final_norm(x2, po_t.T, norm_final_g)

    hd = D // MEM_HEADS
    new_pool_p = proj[:n_p, :W].reshape(Bp, Tp, W)[:, Tp - POOL_BUF:]
    new_pool_s = jnp.concatenate([state_pool[0][:, Ts:], p_s], axis=1)
    return (y[:n_p].reshape(Bp, Tp, D),
            y[n_p:].reshape(Bs, Ts, D),
            new_pool_p[None],
            ret_p[None],
            mk.reshape(1, Bp, M, MEM_HEADS, hd),
            mv.reshape(1, Bp, M, MEM_HEADS, hd),
            new_pool_s[None].astype(state_pool.dtype),
            ret_s[None].astype(state_ret.dtype))
```

```python
import functools

import jax
import jax.numpy as jnp
from jax import lax
from jax.experimental import pallas as pl
from jax.experimental.pallas import tpu as pltpu

F32 = jnp.float32
BF16 = jnp.bfloat16

NORM_EPS = 1e-6
PAST_LEN = 16384
ROPE_BASE = 10000.0
POOL_WINDOWS = (2, 4, 8, 16)
POOL_BUF = max(POOL_WINDOWS) - 1
POOL_PAD = POOL_BUF + 1
RET_HEADS = 4
MEM_HEADS = 4
PEER_HEADS = 8
PEER_TOPK = 16

LANES = 128
SUBLANES = 8
PACK = 16
VMEM_LIMIT = 52 * 1024 * 1024
NEG_INF = float("-inf")


def _params(*sem):
    return pltpu.CompilerParams(dimension_semantics=sem, vmem_limit_bytes=VMEM_LIMIT)


def _pick(n, pref, mult=SUBLANES):
    for c in range(min(pref, n), 0, -1):
        if n % c == 0 and c % mult == 0:
            return c
    raise ValueError(f"no tile for {n} (pref {pref}, mult {mult})")


def _dot(a, b):
    return jnp.dot(a, b, preferred_element_type=F32)


def _dot_nt(a, b):
    return lax.dot_general(a, b, (((1,), (1,)), ((), ())), preferred_element_type=F32)


def _dot_tn(a, b):
    return lax.dot_general(a, b, (((0,), (0,)), ((), ())), preferred_element_type=F32)


def _rms(x, g):
    ms = jnp.mean(x * x, axis=-1, keepdims=True)
    return x * lax.rsqrt(ms + NORM_EPS) * g


def _norm_mm_kernel(x_ref, g_ref, w_ref, o_ref, *rest, emit_h):
    if emit_h:
        h_out_ref, h_ref = rest
    else:
        (h_ref,) = rest

    @pl.when(pl.program_id(1) == 0)
    def _():
        h = _rms(x_ref[...], g_ref[...]).astype(BF16)
        h_ref[...] = h
        if emit_h:
            h_out_ref[...] = h

    o_ref[...] = _dot(h_ref[...], w_ref[...]).astype(o_ref.dtype)


def norm_matmul(x, g, w, *, out_dtype=F32, emit_h=False, tm=1024, tn=512):
    M, K = x.shape
    N = w.shape[1]
    tm, tn = _pick(M, tm), _pick(N, tn, LANES)
    out_shape = [jax.ShapeDtypeStruct((M, N), out_dtype)]
    out_specs = [pl.BlockSpec((tm, tn), lambda i, j: (i, j))]
    if emit_h:
        out_shape.append(jax.ShapeDtypeStruct((M, K), BF16))
        out_specs.append(pl.BlockSpec((tm, K), lambda i, j: (i, 0)))
    res = pl.pallas_call(
        functools.partial(_norm_mm_kernel, emit_h=emit_h),
        grid=(M // tm, N // tn),
        in_specs=[pl.BlockSpec((tm, K), lambda i, j: (i, 0)),
                  pl.BlockSpec((1, K), lambda i, j: (0, 0)),
                  pl.BlockSpec((K, tn), lambda i, j: (0, j))],
        out_specs=out_specs,
        out_shape=out_shape,
        scratch_shapes=[pltpu.VMEM((tm, K), BF16)],
        compiler_params=_params("parallel", "arbitrary"),
        name="norm_matmul",
    )(x, g.reshape(1, K), w)
    return res if emit_h else res[0]


def _res_mm_kernel(*refs, n_lhs):
    res_ref = refs[0]
    a_refs = refs[1:1 + n_lhs]
    w_refs = refs[1 + n_lhs:1 + 2 * n_lhs]
    o_ref = refs[1 + 2 * n_lhs]
    acc = res_ref[...]
    for a_ref, w_ref in zip(a_refs, w_refs):
        acc = acc + _dot(a_ref[...], w_ref[...])
    o_ref[...] = acc


def residual_matmul(res, lhs, ws, *, tm=1024, tn=512):
    M, N = res.shape
    tm, tn = _pick(M, tm), _pick(N, tn, LANES)
    n = len(lhs)
    in_specs = [pl.BlockSpec((tm, tn), lambda i, j: (i, j))]
    in_specs += [pl.BlockSpec((tm, a.shape[1]), lambda i, j: (i, 0)) for a in lhs]
    in_specs += [pl.BlockSpec((w.shape[0], tn), lambda i, j: (0, j)) for w in ws]
    return pl.pallas_call(
        functools.partial(_res_mm_kernel, n_lhs=n),
        grid=(M // tm, N // tn),
        in_specs=in_specs,
        out_specs=pl.BlockSpec((tm, tn), lambda i, j: (i, j)),
        out_shape=jax.ShapeDtypeStruct((M, N), F32),
        compiler_params=_params("parallel", "arbitrary"),
        name="residual_matmul",
    )(res, *lhs, *ws)


def _pool_prompt_kernel(p_ref, pw_ref, sc_ref, a_ref, ext_ref, *, rows):
    T, GD = p_ref.shape
    g = pl.program_id(1)
    ext_ref[0:POOL_PAD, :] = jnp.zeros((POOL_PAD, GD), F32)
    ext_ref[POOL_PAD:, :] = p_ref[...]
    for gi, w in enumerate(POOL_WINDOWS):
        @pl.when(g == gi)
        def _(w=w):
            for r0 in range(0, T, rows):
                acc = ext_ref[POOL_PAD + r0:POOL_PAD + r0 + rows, :]
                tok = acc
                for s in range(1, w):
                    acc = acc + ext_ref[POOL_PAD + r0 - s:POOL_PAD + r0 - s + rows, :]
                t = lax.broadcasted_iota(jnp.int32, (rows, GD), 0) + r0
                cnt = jnp.minimum(t + 1, w).astype(F32)
                d = acc / cnt - tok
                y = _dot(d.astype(BF16), pw_ref[0]) * sc_ref[...]
                a_ref[r0:r0 + rows, :] = y.astype(BF16)


def pool_prompt(proj, pool_w, pool_scale, n_seq, T):
    G, GD, _ = pool_w.shape
    W = G * GD
    return pl.pallas_call(
        functools.partial(_pool_prompt_kernel, rows=_pick(T, 256)),
        grid=(n_seq, G),
        in_specs=[pl.BlockSpec((T, GD), lambda b, g: (b, g)),
                  pl.BlockSpec((1, GD, GD), lambda b, g: (g, 0, 0)),
                  pl.BlockSpec((1, GD), lambda b, g: (0, g))],
        out_specs=pl.BlockSpec((T, GD), lambda b, g: (b, g)),
        out_shape=jax.ShapeDtypeStruct((n_seq * T, W), BF16),
        scratch_shapes=[pltpu.VMEM((POOL_PAD + T, GD), F32)],
        compiler_params=_params("parallel", "arbitrary"),
        name="pool_prompt",
    )(proj, pool_w, pool_scale.reshape(1, W))


def _pool_sample_kernel(pre_ref, p_ref, pw_ref, sc_ref, a_ref):
    Ts = p_ref.shape[0]
    g = pl.program_id(0)

    def ext(r):
        return pre_ref[r] if r < POOL_BUF else p_ref[r - POOL_BUF]

    for gi, w in enumerate(POOL_WINDOWS):
        @pl.when(g == gi)
        def _(w=w):
            for t in range(Ts):
                acc = ext(POOL_BUF + t)
                tok = acc
                for s in range(1, w):
                    acc = acc + ext(POOL_BUF + t - s)
                cnt = float(min(PAST_LEN + t + 1, w))
                d = acc / cnt - tok
                y = _dot(d.astype(BF16), pw_ref[0]) * sc_ref[...]
                a_ref[t] = y.astype(BF16)


def pool_sample(pre_t, p_t, pool_w, pool_scale):
    G, GD, _ = pool_w.shape
    W = G * GD
    Ts, B, _ = p_t.shape
    return pl.pallas_call(
        _pool_sample_kernel,
        grid=(G,),
        in_specs=[pl.BlockSpec((POOL_BUF, B, GD), lambda g: (0, 0, g)),
                  pl.BlockSpec((Ts, B, GD), lambda g: (0, 0, g)),
                  pl.BlockSpec((1, GD, GD), lambda g: (g, 0, 0)),
                  pl.BlockSpec((1, GD), lambda g: (0, g))],
        out_specs=pl.BlockSpec((Ts, B, GD), lambda g: (0, 0, g)),
        out_shape=jax.ShapeDtypeStruct((Ts, B, W), BF16),
        compiler_params=_params("arbitrary"),
        name="pool_sample",
    )(pre_t, p_t, pool_w, pool_scale.reshape(1, W))


def _rotary(x, cos, sin):
    half = x.shape[-1] // 2
    x1, x2 = x[:, :half], x[:, half:]
    return jnp.concatenate([x1 * cos - x2 * sin, x1 * sin + x2 * cos], axis=-1)


def _retention_chunk(q, k, v, gate, cos, sin, gn, s, lg):
    C, dk = q.shape
    qr = _rotary(q, cos, sin).astype(BF16)
    kr = _rotary(k, cos, sin) * (dk ** -0.5)
    vb = v.astype(BF16)
    ri = lax.broadcasted_iota(jnp.int32, (C, C), 0)
    ci = lax.broadcasted_iota(jnp.int32, (C, C), 1)
    diff = (ri - ci).astype(F32)
    decay = jnp.where(diff >= 0, jnp.exp(lg * jnp.maximum(diff, 0.0)), 0.0)
    inner = _dot_nt(qr, kr.astype(BF16)) * decay
    idx = lax.broadcasted_iota(jnp.int32, (C, dk), 0).astype(F32)
    xi = jnp.exp(lg * (idx + 1.0))
    zeta = jnp.exp(lg * (C - 1.0 - idx))
    o = _dot(inner.astype(BF16), vb) + _dot(qr, s.astype(BF16)) * xi
    g_c = jnp.exp(jnp.full((1, s.shape[1]), C, F32) * lg)
    s_new = s * g_c + _dot_tn((kr * zeta).astype(BF16), vb)
    mu = jnp.mean(o, axis=-1, keepdims=True)
    oc = o - mu
    var = jnp.mean(oc * oc, axis=-1, keepdims=True)
    on = oc * lax.rsqrt(var + NORM_EPS) * gn
    return gate * jax.nn.sigmoid(gate) * on, s_new


def _ret_prompt_kernel(lg_ref, q_ref, k_ref, v_ref, gate_ref, cos_ref, sin_ref, gn_ref,
                       b_ref, s_out_ref, s_ref):
    h = pl.program_id(1)
    c = pl.program_id(2)

    @pl.when(c == 0)
    def _():
        s_ref[...] = jnp.zeros(s_ref.shape, F32)

    out, s_new = _retention_chunk(q_ref[...], k_ref[...], v_ref[...], gate_ref[...],
                                  cos_ref[...], sin_ref[...], gn_ref[...], s_ref[...],
                                  lg_ref[h])
    b_ref[...] = out.astype(BF16)
    s_ref[...] = s_new

    @pl.when(c == pl.num_programs(2) - 1)
    def _():
        s_out_ref[0, 0] = s_new


def retention_prompt(proj, cos, sin, ret_norm_g, log_gamma, n_seq, T, width):
    H = RET_HEADS
    dk = width // H
    C = _pick(T, 256)
    nC = T // C

    def col(part):
        return lambda b, h, c: (b * nC + c, part * H + h)

    return pl.pallas_call(
        _ret_prompt_kernel,
        grid=(n_seq, H, nC),
        in_specs=[pl.BlockSpec(memory_space=pltpu.SMEM),
                  pl.BlockSpec((C, dk), col(1)),
                  pl.BlockSpec((C, dk), col(2)),
                  pl.BlockSpec((C, dk), col(3)),
                  pl.BlockSpec((C, dk), col(4)),
                  pl.BlockSpec((C, dk // 2), lambda b, h, c: (c, 0)),
                  pl.BlockSpec((C, dk // 2), lambda b, h, c: (c, 0)),
                  pl.BlockSpec((1, dk), lambda b, h, c: (0, h))],
        out_specs=[pl.BlockSpec((C, dk), lambda b, h, c: (b * nC + c, h)),
                   pl.BlockSpec((1, 1, dk, dk), lambda b, h, c: (b, h, 0, 0))],
        out_shape=[jax.ShapeDtypeStruct((n_seq * T, width), BF16),
                   jax.ShapeDtypeStruct((n_seq, H, dk, dk), F32)],
        scratch_shapes=[pltpu.VMEM((dk, dk), F32)],
        compiler_params=_params("parallel", "parallel", "arbitrary"),
        name="retention_prompt",
    )(log_gamma, proj, proj, proj, proj, cos, sin, ret_norm_g.reshape(1, width))


def _ret_sample_kernel(lg_ref, q_ref, k_ref, v_ref, gate_ref, cos_ref, sin_ref, gn_ref,
                       s_in_ref, b_ref, s_out_ref, *, n_seq, Ts):
    H = RET_HEADS
    dk = q_ref.shape[1] // H
    cos, sin = cos_ref[...], sin_ref[...]
    for b in range(n_seq):
        rows = slice(b * Ts, (b + 1) * Ts)
        for h in range(H):
            cols = slice(h * dk, (h + 1) * dk)
            out, s_new = _retention_chunk(q_ref[rows, cols], k_ref[rows, cols],
                                          v_ref[rows, cols], gate_ref[rows, cols],
                                          cos, sin, gn_ref[:, cols], s_in_ref[b, h],
                                          lg_ref[h])
            b_ref[rows, cols] = out
            s_out_ref[b, h] = s_new


def retention_sample(proj, state, cos, sin, ret_norm_g, log_gamma, row0, n_seq, Ts, width):
    H = RET_HEADS
    dk = width // H
    bb = _pick(n_seq, 4, 1)
    rows = bb * Ts
    blk0 = row0 // rows

    def col(part):
        return lambda i: (blk0 + i, part)

    return pl.pallas_call(
        functools.partial(_ret_sample_kernel, n_seq=bb, Ts=Ts),
        grid=(n_seq // bb,),
        in_specs=[pl.BlockSpec(memory_space=pltpu.SMEM),
                  pl.BlockSpec((rows, width), col(1)),
                  pl.BlockSpec((rows, width), col(2)),
                  pl.BlockSpec((rows, width), col(3)),
                  pl.BlockSpec((rows, width), col(4)),
                  pl.BlockSpec((Ts, dk // 2), lambda i: (0, 0)),
                  pl.BlockSpec((Ts, dk // 2), lambda i: (0, 0)),
                  pl.BlockSpec((1, width), lambda i: (0, 0)),
                  pl.BlockSpec((bb, H, dk, dk), lambda i: (i, 0, 0, 0))],
        out_specs=[pl.BlockSpec((rows, width), lambda i: (i, 0)),
                   pl.BlockSpec((bb, H, dk, dk), lambda i: (i, 0, 0, 0))],
        out_shape=[jax.ShapeDtypeStruct((n_seq * Ts, width), F32),
                   jax.ShapeDtypeStruct((n_seq, H, dk, dk), F32)],
        compiler_params=_params("parallel"),
        name="retention_sample",
    )(log_gamma, proj, proj, proj, proj, cos, sin, ret_norm_g.reshape(1, width), state)


def _attend(q, k, v):
    s = _dot_nt(q.astype(BF16), k.astype(BF16)) * (q.shape[-1] ** -0.5)
    e = jnp.exp(s - jnp.max(s, axis=-1, keepdims=True))
    p = e / jnp.sum(e, axis=-1, keepdims=True)
    return _dot(p.astype(BF16), v.astype(BF16))


def _attn_prompt_kernel(q_ref, k_ref, v_ref, o_ref):
    o_ref[...] = _attend(q_ref[...], k_ref[...], v_ref[...]).astype(BF16)


def mem_attn_prompt(q, mk, mv, n_seq, T):
    D = mk.shape[1]
    M = mk.shape[0] // n_seq
    hd = D // MEM_HEADS
    tq = _pick(T, 512)
    nq = T // tq
    return pl.pallas_call(
        _attn_prompt_kernel,
        grid=(n_seq, nq, MEM_HEADS),
        in_specs=[pl.BlockSpec((tq, hd), lambda b, i, h: (b * nq + i, h)),
                  pl.BlockSpec((M, hd), lambda b, i, h: (b, h)),
                  pl.BlockSpec((M, hd), lambda b, i, h: (b, h))],
        out_specs=pl.BlockSpec((tq, hd), lambda b, i, h: (b * nq + i, h)),
        out_shape=jax.ShapeDtypeStruct((n_seq * T, D), BF16),
        compiler_params=_params("parallel", "parallel", "arbitrary"),
        name="mem_attn_prompt",
    )(q, mk, mv)


def _attn_sample_kernel(q_ref, k_ref, v_ref, o_ref, *, n_seq, Ts):
    _, M, H, hd = k_ref.shape
    col_head = lax.broadcasted_iota(jnp.int32, (H * Ts, M * H), 1) % H
    row_head = lax.broadcasted_iota(jnp.int32, (H * Ts, M * H), 0) // Ts
    own = col_head == row_head
    for b in range(n_seq):
        rows = slice(b * Ts, (b + 1) * Ts)
        q = jnp.concatenate([q_ref[rows, h * hd:(h + 1) * hd] for h in range(H)], axis=0)
        k = k_ref[b].reshape(M * H, hd).astype(BF16)
        v = v_ref[b].reshape(M * H, hd).astype(BF16)
        s = jnp.where(own, _dot_nt(q.astype(BF16), k) * (hd ** -0.5), NEG_INF)
        e = jnp.exp(s - jnp.max(s, axis=-1, keepdims=True))
        p = e / jnp.sum(e, axis=-1, keepdims=True)
        o = _dot(p.astype(BF16), v)
        for h in range(H):
            o_ref[rows, h * hd:(h + 1) * hd] = o[h * Ts:(h + 1) * Ts, :]


def mem_attn_sample(q, ck, cv, row0, n_seq, Ts):
    _, M, H, hd = ck.shape
    D = H * hd
    bb = _pick(n_seq, 2, 1)
    rows = bb * Ts
    blk0 = row0 // rows
    return pl.pallas_call(
        functools.partial(_attn_sample_kernel, n_seq=bb, Ts=Ts),
        grid=(n_seq // bb,),
        in_specs=[pl.BlockSpec((rows, D), lambda i: (blk0 + i, 0)),
                  pl.BlockSpec((bb, M, H, hd), lambda i: (i, 0, 0, 0)),
                  pl.BlockSpec((bb, M, H, hd), lambda i: (i, 0, 0, 0))],
        out_specs=pl.BlockSpec((rows, D), lambda i: (i, 0)),
        out_shape=jax.ShapeDtypeStruct((n_seq * Ts, D), F32),
        compiler_params=_params("parallel"),
        name="mem_attn_sample",
    )(q, ck, cv)


def _top16(s, vals_ref):
    kio = lax.broadcasted_iota(jnp.int32, s.shape, 0).astype(F32)
    rank = jnp.full(s.shape, float(PEER_TOPK), F32)
    cur = s
    for a in range(PEER_TOPK):
        m = jnp.max(cur, axis=0, keepdims=True)
        first = jnp.min(jnp.where(cur == m, kio, float(s.shape[0])), axis=0, keepdims=True)
        hit = kio == first
        rank = jnp.where(hit, float(a), rank)
        cur = jnp.where(hit, NEG_INF, cur)
        vals_ref[a:a + 1, :] = m
    return rank


def _candidate_grid(v1_ref, v2_ref):
    K = PEER_TOPK
    L = v1_ref.shape[1]
    v2_lo = v2_ref[0:SUBLANES, :]
    bio = lax.broadcasted_iota(jnp.int32, (SUBLANES, L), 0)
    bio_f = bio.astype(F32)
    cands = [v1_ref[0:1, :] + v2_ref[...]]
    cidx = [lax.broadcasted_iota(jnp.int32, (K, L), 0).astype(F32)]
    pieces = [(0, K, 0)]
    row = K
    for a in range(1, SUBLANES):
        limit = K // (a + 1)
        c = v1_ref[a:a + 1, :] + v2_lo
        if limit < SUBLANES:
            c = jnp.where(bio < limit, c, NEG_INF)
        cands.append(c)
        cidx.append(bio_f + float(a * K))
        pieces.append((row, SUBLANES, a))
        row += SUBLANES
    cands.append(v1_ref[SUBLANES:K, :] + v2_ref[0:1, :])
    cidx.append((bio_f + float(SUBLANES)) * float(K))
    pieces.append((row, SUBLANES, None))
    return jnp.concatenate(cands, axis=0), jnp.concatenate(cidx, axis=0), pieces


def _peer_route_kernel(q_ref, sk_ref, e1_ref, n_ref, r2_ref, e2_ref, v1_ref, v2_ref):
    T = q_ref.shape[0]
    half = sk_ref.shape[3]
    K = PEER_TOPK
    for t0 in range(0, T, LANES):
        lanes = slice(t0, t0 + LANES)
        q = q_ref[lanes, :].astype(BF16)
        s1 = _dot_nt(sk_ref[0, 0], q[:, :half])
        s2 = _dot_nt(sk_ref[0, 1], q[:, half:])
        r1 = _top16(s1, v1_ref)
        r2 = _top16(s2, v2_ref)

        cand, cidx, pieces = _candidate_grid(v1_ref, v2_ref)
        cur = cand
        sel = jnp.zeros(cand.shape, F32)
        for _ in range(K):
            m = jnp.max(cur, axis=0, keepdims=True)
            first = jnp.min(jnp.where(cur == m, cidx, float(K * K)), axis=0, keepdims=True)
            hit = cidx == first
            sel = jnp.where(hit, 1.0, sel)
            cur = jnp.where(hit, NEG_INF, cur)

        top = v1_ref[0:1, :] + v2_ref[0:1, :]
        z = jnp.sum(sel * jnp.exp(cand - top), axis=0, keepdims=True)
        n = jnp.zeros(s1.shape, F32)
        for row0, rows, a in pieces:
            piece = sel[row0:row0 + rows, :]
            if a is None:
                for r in range(rows):
                    n = jnp.where(r1 == float(SUBLANES + r), piece[r:r + 1, :], n)
            else:
                n = jnp.where(r1 == float(a), jnp.sum(piece, axis=0, keepdims=True), n)

        e1 = jnp.where(r1 < float(K), jnp.exp(s1 - v1_ref[0:1, :]), 0.0) / z
        e1_ref[0, :, lanes] = 0.5 * e1
        n_ref[0, :, lanes] = n
        r2_ref[:, lanes] = r2.astype(BF16)
        e2_ref[:, lanes] = jnp.exp(s2 - v2_ref[0:1, :]).astype(BF16)


def peer_route(qp, subkeys, tile):
    n_tok = qp.shape[0]
    heads, _, n_keys, half = subkeys.shape
    spec3 = pl.BlockSpec((1, n_keys, tile), lambda i, h: (h, 0, i))
    shape3 = jax.ShapeDtypeStruct((heads, n_keys, n_tok), F32)
    spec2 = pl.BlockSpec((n_keys, tile), lambda i, h: (h, i))
    shape2 = jax.ShapeDtypeStruct((heads * n_keys, n_tok), BF16)
    return pl.pallas_call(
        _peer_route_kernel,
        grid=(n_tok // tile, heads),
        in_specs=[pl.BlockSpec((tile, 2 * half), lambda i, h: (i, h)),
                  pl.BlockSpec((1, 2, n_keys, half), lambda i, h: (h, 0, 0, 0))],
        out_specs=[spec3, spec3, spec2, spec2],
        out_shape=[shape3, shape3, shape2, shape2],
        scratch_shapes=[pltpu.VMEM((PEER_TOPK, LANES), F32)] * 2,
        compiler_params=_params("parallel", "arbitrary"),
        name="peer_route",
    )(qp, subkeys)


def _peer_dense_kernel(xt_ref, u_ref, vt_ref, e1_ref, n_ref, r2_in_ref, e2_in_ref, o_ref,
                       r2_ref, e2_ref, nb_ref, eb_ref, g_ref, a_ref, *, chunk):
    eb = pl.program_id(1)
    heads, per_blk, T = e1_ref.shape
    n_keys = r2_in_ref.shape[0] // heads
    E = u_ref.shape[0]

    @pl.when(eb == 0)
    def _():
        o_ref[...] = jnp.zeros(o_ref.shape, F32)
        r2_ref[...] = r2_in_ref[...]
        e2_ref[...] = e2_in_ref[...]

    def key_rows(h, ii):
        r0 = (h * per_blk + ii) * PACK
        return slice(r0, r0 + PACK)

    for h in range(heads):
        for ii in range(per_blk):
            nb_ref[key_rows(h, ii), :] = jnp.broadcast_to(
                n_ref[h, ii:ii + 1, :], (PACK, T)).astype(BF16)
            eb_ref[key_rows(h, ii), :] = jnp.broadcast_to(
                e1_ref[h, ii:ii + 1, :], (PACK, T)).astype(BF16)

    zero = jnp.zeros((PACK, LANES), BF16)
    groups = n_keys // PACK
    for c0 in range(0, E, chunk):
        for ii in range(c0 // n_keys, (c0 + chunk) // n_keys):
            for t0 in range(0, T, LANES):
                lanes = slice(t0, t0 + LANES)
                acc = [zero] * groups
                for h in range(heads):
                    n = nb_ref[key_rows(h, ii), lanes]
                    e1 = eb_ref[key_rows(h, ii), lanes]
                    for jj in range(groups):
                        rows = slice(h * n_keys + jj * PACK, h * n_keys + (jj + 1) * PACK)
                        acc[jj] = acc[jj] + jnp.where(r2_ref[rows, lanes] < n,
                                                      e2_ref[rows, lanes], zero) * e1
                for jj in range(groups):
                    r0 = ii * n_keys + jj * PACK
                    g_ref[r0:r0 + PACK, lanes] = acc[jj]
        rows = slice(c0, c0 + chunk)
        pre = _dot(u_ref[rows, :], xt_ref[...])
        act = pre * (1.0 + lax.erf(pre * (2.0 ** -0.5)))
        a_ref[rows, :] = act.astype(BF16) * g_ref[rows, :]
    o_ref[...] += _dot(vt_ref[...], a_ref[...])


def peer_dense(xt, u, vt, tables, tile, e_blk):
    e1, n, r2, e2 = tables
    D, n_tok = xt.shape
    n_exp = u.shape[0]
    heads, n_keys, _ = e1.shape
    per_blk = e_blk // n_keys
    assert e_blk % (SUBLANES * n_keys) == 0 and n_keys % PACK == 0
    key_spec = pl.BlockSpec((heads * n_keys, tile), lambda i, e: (0, i))
    blk_spec = pl.BlockSpec((heads, per_blk, tile), lambda i, e: (0, e, i))
    return pl.pallas_call(
        functools.partial(_peer_dense_kernel, chunk=_pick(e_blk, 256, n_keys)),
        grid=(n_tok // tile, n_exp // e_blk),
        in_specs=[pl.BlockSpec((D, tile), lambda i, e: (0, i)),
                  pl.BlockSpec((e_blk, D), lambda i, e: (e, 0)),
                  pl.BlockSpec((D, e_blk), lambda i, e: (0, e)),
                  blk_spec, blk_spec, key_spec, key_spec],
        out_specs=pl.BlockSpec((D, tile), lambda i, e: (0, i)),
        out_shape=jax.ShapeDtypeStruct((D, n_tok), F32),
        scratch_shapes=[pltpu.VMEM((heads * n_keys, tile), BF16),
                        pltpu.VMEM((heads * n_keys, tile), BF16),
                        pltpu.VMEM((heads * per_blk * PACK, tile), BF16),
                        pltpu.VMEM((heads * per_blk * PACK, tile), BF16),
                        pltpu.VMEM((e_blk, tile), BF16),
                        pltpu.VMEM((e_blk, tile), BF16)],
        compiler_params=_params("parallel", "arbitrary"),
        name="peer_dense",
    )(xt, u, vt, e1, n, r2, e2)


def _final_kernel(x_ref, p_ref, g_ref, o_ref):
    o_ref[...] = _rms(x_ref[...] + p_ref[...], g_ref[...])


def final_norm(x, p, g, tm=512):
    M, D = x.shape
    tm = _pick(M, tm)
    spec = pl.BlockSpec((tm, D), lambda i: (i, 0))
    return pl.pallas_call(
        _final_kernel,
        grid=(M // tm,),
        in_specs=[spec, spec, pl.BlockSpec((1, D), lambda i: (0, 0))],
        out_specs=spec,
        out_shape=jax.ShapeDtypeStruct((M, D), F32),
        compiler_params=_params("parallel"),
        name="final_norm",
    )(x, p, g.reshape(1, D))


def _rope_tables(pos, half):
    inv = 1.0 / (ROPE_BASE ** (jnp.arange(half, dtype=F32) / half))
    ang = pos.astype(F32)[:, None] * inv[None, :]
    return jnp.cos(ang), jnp.sin(ang)


def kernel(x_prompt, x_sample, state_pool, state_ret, cache_mem_k, cache_mem_v, mem_prompt,
           norm_mix_g, w_in, pool_w, pool_scale, ret_norm_g, w_out, norm_mem_g,
           norm_memsrc_g, w_mq, w_mk, w_mv, w_mo, norm_ffn_g, peer_wq, peer_subkeys,
           peer_u, peer_v, norm_final_g):
    Bp, Tp, D = x_prompt.shape
    Bs, Ts, _ = x_sample.shape
    depth = w_in.shape[0]
    assert depth == 1, "single-layer step"
    n_p, n_s = Bp * Tp, Bs * Ts
    n_tok = n_p + n_s
    W = pool_w.shape[1] * pool_w.shape[2]
    dk = W // RET_HEADS
    M = mem_prompt.shape[1]
    n_keys = peer_subkeys.shape[3]
    assert POOL_BUF >= Ts and w_in.shape[2] == 5 * W

    bf = lambda a: a.astype(BF16)
    x0 = jnp.concatenate([x_prompt.reshape(n_p, D), x_sample.reshape(n_s, D)], axis=0)

    proj = norm_matmul(x0, norm_mix_g[0], bf(w_in[0]))
    a_p = pool_prompt(proj, bf(pool_w[0]), pool_scale[0], Bp, Tp)
    p_s = proj[n_p:, :W].reshape(Bs, Ts, W)
    a_s = pool_sample(state_pool[0].transpose(1, 0, 2), p_s.transpose(1, 0, 2),
                      bf(pool_w[0]), pool_scale[0]).transpose(1, 0, 2).reshape(n_s, W)

    log_gamma = jnp.log(1.0 - 2.0 ** (-5.0 - jnp.arange(RET_HEADS, dtype=F32)))
    cos_p, sin_p = _rope_tables(jnp.arange(Tp, dtype=jnp.int32), dk // 2)
    cos_s, sin_s = _rope_tables(PAST_LEN + jnp.arange(Ts, dtype=jnp.int32), dk // 2)
    b_p, ret_p = retention_prompt(proj, cos_p, sin_p, ret_norm_g[0], log_gamma, Bp, Tp, W)
    b_s, ret_s = retention_sample(proj, state_ret[0], cos_s, sin_s, ret_norm_g[0],
                                  log_gamma, n_p, Bs, Ts, W)

    w_o = bf(w_out[0])
    x1 = residual_matmul(x0, [jnp.concatenate([a_p, a_s], axis=0),
                              jnp.concatenate([b_p, bf(b_s)], axis=0)], [w_o[:W], w_o[W:]])

    mem = mem_prompt.reshape(Bp * M, D)
    mk = norm_matmul(mem, norm_memsrc_g[0], bf(w_mk[0]))
    mv = norm_matmul(mem, norm_memsrc_g[0], bf(w_mv[0]))
    qm = norm_matmul(x1, norm_mem_g[0], bf(w_mq[0]))
    o_p = mem_attn_prompt(qm, mk, mv, Bp, Tp)
    o_s = mem_attn_sample(qm, cache_mem_k.reshape(cache_mem_k.shape[1:]),
                          cache_mem_v.reshape(cache_mem_v.shape[1:]), n_p, Bs, Ts)
    x2 = residual_matmul(x1, [jnp.concatenate([o_p, bf(o_s)], axis=0)], [bf(w_mo[0])])

    qp, h3 = norm_matmul(x2, norm_ffn_g[0], bf(peer_wq[0]), emit_h=True)
    tile = _pick(n_tok, 512, LANES)
    tables = peer_route(qp, bf(peer_subkeys[0]), tile)
    e_blk = SUBLANES * n_keys
    po_t = peer_dense(h3.T, bf(peer_u[0]), bf(peer_v[0]).T, tables, tile, e_blk)
    y = final_norm(x2, po_t.T, norm_final_g)

    hd = D // MEM_HEADS
    new_pool_p = proj[:n_p, :W].reshape(Bp, Tp, W)[:, Tp - POOL_BUF:]
    new_pool_s = jnp.concatenate([state_pool[0][:, Ts:], p_s], axis=1)
    return (y[:n_p].reshape(Bp, Tp, D),
            y[n_p:].reshape(Bs, Ts, D),
            new_pool_p[None],
            ret_p[None],
            mk.reshape(1, Bp, M, MEM_HEADS, hd),
            mv.reshape(1, Bp, M, MEM_HEADS, hd),
            new_pool_s[None].astype(state_pool.dtype),
            ret_s[None].astype(state_ret.dtype))
```

```python
import functools

import jax
import jax.numpy as jnp
from jax import lax
from jax.experimental import pallas as pl
from jax.experimental.pallas import tpu as pltpu

F32 = jnp.float32
BF16 = jnp.bfloat16

NORM_EPS = 1e-6
PAST_LEN = 16384
ROPE_BASE = 10000.0
POOL_WINDOWS = (2, 4, 8, 16)
POOL_BUF = max(POOL_WINDOWS) - 1
POOL_PAD = POOL_BUF + 1
RET_HEADS = 4
MEM_HEADS = 4
PEER_HEADS = 8
PEER_TOPK = 16

LANES = 128
SUBLANES = 8
VMEM_LIMIT = 52 * 1024 * 1024
NEG_INF = float("-inf")


def _params(*sem):
    return pltpu.CompilerParams(dimension_semantics=sem, vmem_limit_bytes=VMEM_LIMIT)


def _pick(n, pref, mult=SUBLANES):
    for c in range(min(pref, n), 0, -1):
        if n % c == 0 and c % mult == 0:
            return c
    raise ValueError(f"no tile for {n} (pref {pref}, mult {mult})")


def _dot(a, b):
    return jnp.dot(a, b, preferred_element_type=F32)


def _dot_nt(a, b):
    return lax.dot_general(a, b, (((1,), (1,)), ((), ())), preferred_element_type=F32)


def _dot_tn(a, b):
    return lax.dot_general(a, b, (((0,), (0,)), ((), ())), preferred_element_type=F32)


def _rms(x, g):
    ms = jnp.mean(x * x, axis=-1, keepdims=True)
    return x * lax.rsqrt(ms + NORM_EPS) * g


def _norm_mm_kernel(x_ref, g_ref, w_ref, o_ref, *rest, emit_h):
    if emit_h:
        h_out_ref, h_ref = rest
    else:
        (h_ref,) = rest

    @pl.when(pl.program_id(1) == 0)
    def _():
        h = _rms(x_ref[...], g_ref[...]).astype(BF16)
        h_ref[...] = h
        if emit_h:
            h_out_ref[...] = h

    o_ref[...] = _dot(h_ref[...], w_ref[...]).astype(o_ref.dtype)


def norm_matmul(x, g, w, *, out_dtype=F32, emit_h=False, tm=1024, tn=1024):
    M, K = x.shape
    N = w.shape[1]
    tm, tn = _pick(M, tm), _pick(N, tn, LANES)
    out_shape = [jax.ShapeDtypeStruct((M, N), out_dtype)]
    out_specs = [pl.BlockSpec((tm, tn), lambda i, j: (i, j))]
    if emit_h:
        out_shape.append(jax.ShapeDtypeStruct((M, K), BF16))
        out_specs.append(pl.BlockSpec((tm, K), lambda i, j: (i, 0)))
    res = pl.pallas_call(
        functools.partial(_norm_mm_kernel, emit_h=emit_h),
        grid=(M // tm, N // tn),
        in_specs=[pl.BlockSpec((tm, K), lambda i, j: (i, 0)),
                  pl.BlockSpec((1, K), lambda i, j: (0, 0)),
                  pl.BlockSpec((K, tn), lambda i, j: (0, j))],
        out_specs=out_specs,
        out_shape=out_shape,
        scratch_shapes=[pltpu.VMEM((tm, K), BF16)],
        compiler_params=_params("parallel", "arbitrary"),
        name="norm_matmul",
    )(x, g.reshape(1, K), w)
    return res if emit_h else res[0]


def _res_mm_kernel(*refs, n_lhs):
    res_ref = refs[0]
    a_refs = refs[1:1 + n_lhs]
    w_refs = refs[1 + n_lhs:1 + 2 * n_lhs]
    o_ref = refs[1 + 2 * n_lhs]
    acc = res_ref[...]
    for a_ref, w_ref in zip(a_refs, w_refs):
        acc = acc + _dot(a_ref[...], w_ref[...])
    o_ref[...] = acc


def residual_matmul(res, lhs, ws, *, tm=1024, tn=1024):
    M, N = res.shape
    tm, tn = _pick(M, tm), _pick(N, tn, LANES)
    n = len(lhs)
    in_specs = [pl.BlockSpec((tm, tn), lambda i, j: (i, j))]
    in_specs += [pl.BlockSpec((tm, a.shape[1]), lambda i, j: (i, 0)) for a in lhs]
    in_specs += [pl.BlockSpec((w.shape[0], tn), lambda i, j: (0, j)) for w in ws]
    return pl.pallas_call(
        functools.partial(_res_mm_kernel, n_lhs=n),
        grid=(M // tm, N // tn),
        in_specs=in_specs,
        out_specs=pl.BlockSpec((tm, tn), lambda i, j: (i, j)),
        out_shape=jax.ShapeDtypeStruct((M, N), F32),
        compiler_params=_params("parallel", "arbitrary"),
        name="residual_matmul",
    )(res, *lhs, *ws)


def _pool_prompt_kernel(p_ref, pw_ref, sc_ref, a_ref, ext_ref, *, rows):
    T, GD = p_ref.shape
    g = pl.program_id(1)
    ext_ref[0:POOL_PAD, :] = jnp.zeros((POOL_PAD, GD), F32)
    ext_ref[POOL_PAD:, :] = p_ref[...]
    for gi, w in enumerate(POOL_WINDOWS):
        @pl.when(g == gi)
        def _(w=w):
            for r0 in range(0, T, rows):
                acc = ext_ref[POOL_PAD + r0:POOL_PAD + r0 + rows, :]
                tok = acc
                for s in range(1, w):
                    acc = acc + ext_ref[POOL_PAD + r0 - s:POOL_PAD + r0 - s + rows, :]
                t = lax.broadcasted_iota(jnp.int32, (rows, GD), 0) + r0
                cnt = jnp.minimum(t + 1, w).astype(F32)
                d = acc / cnt - tok
                y = _dot(d.astype(BF16), pw_ref[0]) * sc_ref[...]
                a_ref[r0:r0 + rows, :] = y.astype(BF16)


def pool_prompt(proj, pool_w, pool_scale, n_seq, T):
    G, GD, _ = pool_w.shape
    W = G * GD
    return pl.pallas_call(
        functools.partial(_pool_prompt_kernel, rows=_pick(T, 256)),
        grid=(n_seq, G),
        in_specs=[pl.BlockSpec((T, GD), lambda b, g: (b, g)),
                  pl.BlockSpec((1, GD, GD), lambda b, g: (g, 0, 0)),
                  pl.BlockSpec((1, GD), lambda b, g: (0, g))],
        out_specs=pl.BlockSpec((T, GD), lambda b, g: (b, g)),
        out_shape=jax.ShapeDtypeStruct((n_seq * T, W), BF16),
        scratch_shapes=[pltpu.VMEM((POOL_PAD + T, GD), F32)],
        compiler_params=_params("parallel", "arbitrary"),
        name="pool_prompt",
    )(proj, pool_w, pool_scale.reshape(1, W))


def _pool_sample_kernel(pre_ref, p_ref, pw_ref, sc_ref, a_ref):
    Ts = p_ref.shape[0]
    g = pl.program_id(0)

    def ext(r):
        return pre_ref[r] if r < POOL_BUF else p_ref[r - POOL_BUF]

    for gi, w in enumerate(POOL_WINDOWS):
        @pl.when(g == gi)
        def _(w=w):
            for t in range(Ts):
                acc = ext(POOL_BUF + t)
                tok = acc
                for s in range(1, w):
                    acc = acc + ext(POOL_BUF + t - s)
                cnt = float(min(PAST_LEN + t + 1, w))
                d = acc / cnt - tok
                y = _dot(d.astype(BF16), pw_ref[0]) * sc_ref[...]
                a_ref[t] = y.astype(BF16)


def pool_sample(pre_t, p_t, pool_w, pool_scale):
    G, GD, _ = pool_w.shape
    W = G * GD
    Ts, B, _ = p_t.shape
    return pl.pallas_call(
        _pool_sample_kernel,
        grid=(G,),
        in_specs=[pl.BlockSpec((POOL_BUF, B, GD), lambda g: (0, 0, g)),
                  pl.BlockSpec((Ts, B, GD), lambda g: (0, 0, g)),
                  pl.BlockSpec((1, GD, GD), lambda g: (g, 0, 0)),
                  pl.BlockSpec((1, GD), lambda g: (0, g))],
        out_specs=pl.BlockSpec((Ts, B, GD), lambda g: (0, 0, g)),
        out_shape=jax.ShapeDtypeStruct((Ts, B, W), BF16),
        compiler_params=_params("arbitrary"),
        name="pool_sample",
    )(pre_t, p_t, pool_w, pool_scale.reshape(1, W))


def _rotary(x, cos, sin):
    half = x.shape[-1] // 2
    x1, x2 = x[:, :half], x[:, half:]
    return jnp.concatenate([x1 * cos - x2 * sin, x1 * sin + x2 * cos], axis=-1)


def _retention_chunk(q, k, v, gate, cos, sin, gn, s, lg):
    C, dk = q.shape
    qr = _rotary(q, cos, sin).astype(BF16)
    kr = _rotary(k, cos, sin) * (dk ** -0.5)
    vb = v.astype(BF16)
    ri = lax.broadcasted_iota(jnp.int32, (C, C), 0)
    ci = lax.broadcasted_iota(jnp.int32, (C, C), 1)
    diff = (ri - ci).astype(F32)
    decay = jnp.where(diff >= 0, jnp.exp(lg * jnp.maximum(diff, 0.0)), 0.0)
    inner = _dot_nt(qr, kr.astype(BF16)) * decay
    idx = lax.broadcasted_iota(jnp.int32, (C, dk), 0).astype(F32)
    xi = jnp.exp(lg * (idx + 1.0))
    zeta = jnp.exp(lg * (C - 1.0 - idx))
    o = _dot(inner.astype(BF16), vb) + _dot(qr, s.astype(BF16)) * xi
    g_c = jnp.exp(jnp.full((1, s.shape[1]), C, F32) * lg)
    s_new = s * g_c + _dot_tn((kr * zeta).astype(BF16), vb)
    mu = jnp.mean(o, axis=-1, keepdims=True)
    oc = o - mu
    var = jnp.mean(oc * oc, axis=-1, keepdims=True)
    on = oc * lax.rsqrt(var + NORM_EPS) * gn
    return gate * jax.nn.sigmoid(gate) * on, s_new


def _ret_prompt_kernel(lg_ref, q_ref, k_ref, v_ref, gate_ref, cos_ref, sin_ref, gn_ref,
                       b_ref, s_out_ref, s_ref):
    h = pl.program_id(1)
    c = pl.program_id(2)

    @pl.when(c == 0)
    def _():
        s_ref[...] = jnp.zeros(s_ref.shape, F32)

    out, s_new = _retention_chunk(q_ref[...], k_ref[...], v_ref[...], gate_ref[...],
                                  cos_ref[...], sin_ref[...], gn_ref[...], s_ref[...],
                                  lg_ref[h])
    b_ref[...] = out.astype(BF16)
    s_ref[...] = s_new

    @pl.when(c == pl.num_programs(2) - 1)
    def _():
        s_out_ref[0, 0] = s_new


def retention_prompt(proj, cos, sin, ret_norm_g, log_gamma, n_seq, T, width):
    H = RET_HEADS
    dk = width // H
    C = _pick(T, 256)
    nC = T // C

    def col(part):
        return lambda b, h, c: (b * nC + c, part * H + h)

    return pl.pallas_call(
        _ret_prompt_kernel,
        grid=(n_seq, H, nC),
        in_specs=[pl.BlockSpec(memory_space=pltpu.SMEM),
                  pl.BlockSpec((C, dk), col(1)),
                  pl.BlockSpec((C, dk), col(2)),
                  pl.BlockSpec((C, dk), col(3)),
                  pl.BlockSpec((C, dk), col(4)),
                  pl.BlockSpec((C, dk // 2), lambda b, h, c: (c, 0)),
                  pl.BlockSpec((C, dk // 2), lambda b, h, c: (c, 0)),
                  pl.BlockSpec((1, dk), lambda b, h, c: (0, h))],
        out_specs=[pl.BlockSpec((C, dk), lambda b, h, c: (b * nC + c, h)),
                   pl.BlockSpec((1, 1, dk, dk), lambda b, h, c: (b, h, 0, 0))],
        out_shape=[jax.ShapeDtypeStruct((n_seq * T, width), BF16),
                   jax.ShapeDtypeStruct((n_seq, H, dk, dk), F32)],
        scratch_shapes=[pltpu.VMEM((dk, dk), F32)],
        compiler_params=_params("parallel", "parallel", "arbitrary"),
        name="retention_prompt",
    )(log_gamma, proj, proj, proj, proj, cos, sin, ret_norm_g.reshape(1, width))


def _ret_sample_kernel(lg_ref, q_ref, k_ref, v_ref, gate_ref, cos_ref, sin_ref, gn_ref,
                       s_in_ref, b_ref, s_out_ref, *, n_seq, Ts):
    H = RET_HEADS
    dk = q_ref.shape[1] // H
    cos, sin = cos_ref[...], sin_ref[...]
    for b in range(n_seq):
        rows = slice(b * Ts, (b + 1) * Ts)
        for h in range(H):
            cols = slice(h * dk, (h + 1) * dk)
            out, s_new = _retention_chunk(q_ref[rows, cols], k_ref[rows, cols],
                                          v_ref[rows, cols], gate_ref[rows, cols],
                                          cos, sin, gn_ref[:, cols], s_in_ref[b, h],
                                          lg_ref[h])
            b_ref[rows, cols] = out
            s_out_ref[b, h] = s_new


def retention_sample(proj, state, cos, sin, ret_norm_g, log_gamma, n_seq, Ts, width):
    H = RET_HEADS
    dk = width // H
    bb = _pick(n_seq, 4, 1)
    rows = bb * Ts

    def col(part):
        return lambda i: (i, part)

    return pl.pallas_call(
        functools.partial(_ret_sample_kernel, n_seq=bb, Ts=Ts),
        grid=(n_seq // bb,),
        in_specs=[pl.BlockSpec(memory_space=pltpu.SMEM),
                  pl.BlockSpec((rows, width), col(1)),
                  pl.BlockSpec((rows, width), col(2)),
                  pl.BlockSpec((rows, width), col(3)),
                  pl.BlockSpec((rows, width), col(4)),
                  pl.BlockSpec((Ts, dk // 2), lambda i: (0, 0)),
                  pl.BlockSpec((Ts, dk // 2), lambda i: (0, 0)),
                  pl.BlockSpec((1, width), lambda i: (0, 0)),
                  pl.BlockSpec((bb, H, dk, dk), lambda i: (i, 0, 0, 0))],
        out_specs=[pl.BlockSpec((rows, width), lambda i: (i, 0)),
                   pl.BlockSpec((bb, H, dk, dk), lambda i: (i, 0, 0, 0))],
        out_shape=[jax.ShapeDtypeStruct((n_seq * Ts, width), F32),
                   jax.ShapeDtypeStruct((n_seq, H, dk, dk), F32)],
        compiler_params=_params("parallel"),
        name="retention_sample",
    )(log_gamma, proj, proj, proj, proj, cos, sin, ret_norm_g.reshape(1, width), state)


def _attend(q, k, v):
    s = _dot_nt(q.astype(BF16), k.astype(BF16)) * (q.shape[-1] ** -0.5)
    e = jnp.exp(s - jnp.max(s, axis=-1, keepdims=True))
    p = e / jnp.sum(e, axis=-1, keepdims=True)
    return _dot(p.astype(BF16), v.astype(BF16))


def _attn_prompt_kernel(q_ref, k_ref, v_ref, o_ref):
    o_ref[...] = _attend(q_ref[...], k_ref[...], v_ref[...]).astype(BF16)


def mem_attn_prompt(q, mk, mv, n_seq, T):
    D = mk.shape[1]
    M = mk.shape[0] // n_seq
    hd = D // MEM_HEADS
    tq = _pick(T, 512)
    nq = T // tq
    return pl.pallas_call(
        _attn_prompt_kernel,
        grid=(n_seq, nq, MEM_HEADS),
        in_specs=[pl.BlockSpec((tq, hd), lambda b, i, h: (b * nq + i, h)),
                  pl.BlockSpec((M, hd), lambda b, i, h: (b, h)),
                  pl.BlockSpec((M, hd), lambda b, i, h: (b, h))],
        out_specs=pl.BlockSpec((tq, hd), lambda b, i, h: (b * nq + i, h)),
        out_shape=jax.ShapeDtypeStruct((n_seq * T, D), BF16),
        compiler_params=_params("parallel", "parallel", "arbitrary"),
        name="mem_attn_prompt",
    )(q, mk, mv)


def _attn_sample_kernel(q_ref, k_ref, v_ref, o_ref, *, n_seq, Ts):
    _, M, H, hd = k_ref.shape
    col_head = lax.broadcasted_iota(jnp.int32, (H * Ts, M * H), 1) % H
    row_head = lax.broadcasted_iota(jnp.int32, (H * Ts, M * H), 0) // Ts
    own = col_head == row_head
    for b in range(n_seq):
        rows = slice(b * Ts, (b + 1) * Ts)
        q = jnp.concatenate([q_ref[rows, h * hd:(h + 1) * hd] for h in range(H)], axis=0)
        k = k_ref[b].reshape(M * H, hd).astype(BF16)
        v = v_ref[b].reshape(M * H, hd).astype(BF16)
        s = jnp.where(own, _dot_nt(q.astype(BF16), k) * (hd ** -0.5), NEG_INF)
        e = jnp.exp(s - jnp.max(s, axis=-1, keepdims=True))
        p = e / jnp.sum(e, axis=-1, keepdims=True)
        o = _dot(p.astype(BF16), v)
        for h in range(H):
            o_ref[rows, h * hd:(h + 1) * hd] = o[h * Ts:(h + 1) * Ts, :]


def mem_attn_sample(q, ck, cv, n_seq, Ts):
    _, M, H, hd = ck.shape
    D = H * hd
    bb = _pick(n_seq, 2, 1)
    rows = bb * Ts
    return pl.pallas_call(
        functools.partial(_attn_sample_kernel, n_seq=bb, Ts=Ts),
        grid=(n_seq // bb,),
        in_specs=[pl.BlockSpec((rows, D), lambda i: (i, 0)),
                  pl.BlockSpec((bb, M, H, hd), lambda i: (i, 0, 0, 0)),
                  pl.BlockSpec((bb, M, H, hd), lambda i: (i, 0, 0, 0))],
        out_specs=pl.BlockSpec((rows, D), lambda i: (i, 0)),
        out_shape=jax.ShapeDtypeStruct((n_seq * Ts, D), F32),
        compiler_params=_params("parallel"),
        name="mem_attn_sample",
    )(q, ck, cv)


def _extract16(cur, idx, exact):
    for a in range(PEER_TOPK):
        m = jnp.max(cur, axis=0, keepdims=True)
        hit = cur == m
        if exact:
            first = jnp.min(jnp.where(hit, idx, float(cur.shape[0] * PEER_TOPK)),
                            axis=0, keepdims=True)
            hit = idx == first
        cur = jnp.where(hit, NEG_INF, cur)
        yield a, m, hit


def _top16(s, vals_ref, exact):
    kio = lax.broadcasted_iota(jnp.int32, s.shape, 0).astype(F32)
    rank = jnp.full(s.shape, float(PEER_TOPK), F32)
    for a, m, hit in _extract16(s, kio, exact):
        rank = jnp.where(hit, float(a), rank)
        vals_ref[a:a + 1, :] = m
    return rank


def _candidate_grid(v1_ref, v2_ref):
    K = PEER_TOPK
    L = v1_ref.shape[1]
    v2_lo = v2_ref[0:SUBLANES, :]
    bio = lax.broadcasted_iota(jnp.int32, (SUBLANES, L), 0)
    bio_f = bio.astype(F32)
    cands = [v1_ref[0:1, :] + v2_ref[...]]
    cidx = [lax.broadcasted_iota(jnp.int32, (K, L), 0).astype(F32)]
    pieces = [(0, K, 0)]
    row = K
    for a in range(1, SUBLANES):
        limit = K // (a + 1)
        c = v1_ref[a:a + 1, :] + v2_lo
        if limit < SUBLANES:
            c = jnp.where(bio < limit, c, NEG_INF)
        cands.append(c)
        cidx.append(bio_f + float(a * K))
        pieces.append((row, SUBLANES, a))
        row += SUBLANES
    cands.append(v1_ref[SUBLANES:K, :] + v2_ref[0:1, :])
    cidx.append((bio_f + float(SUBLANES)) * float(K))
    pieces.append((row, SUBLANES, None))
    return jnp.concatenate(cands, axis=0), jnp.concatenate(cidx, axis=0), pieces


def _peer_route_kernel(q_ref, sk_ref, e1_ref, n_ref, r2_ref, e2_ref, vals_ref):
    T = q_ref.shape[0]
    half = sk_ref.shape[3]
    K = PEER_TOPK

    def route(s1, s2, ti, v1_ref, v2_ref, exact):
        r1 = _top16(s1, v1_ref, exact)
        r2 = _top16(s2, v2_ref, exact)
        cand, cidx, pieces = _candidate_grid(v1_ref, v2_ref)
        sel = jnp.zeros(cand.shape, F32)
        for _, _, hit in _extract16(cand, cidx, exact):
            sel = jnp.where(hit, 1.0, sel)

        top = v1_ref[0:1, :] + v2_ref[0:1, :]
        z = jnp.sum(sel * jnp.exp(cand - top), axis=0, keepdims=True)
        n = jnp.zeros(s1.shape, F32)
        for row0, rows, a in pieces:
            piece = sel[row0:row0 + rows, :]
            if a is None:
                for r in range(rows):
                    n = jnp.where(r1 == float(SUBLANES + r), piece[r:r + 1, :], n)
            else:
                n = jnp.where(r1 == float(a), jnp.sum(piece, axis=0, keepdims=True), n)

        in1 = r1 < float(K)
        e1_ref[0, ti] = 0.5 * jnp.where(in1, jnp.exp(s1 - v1_ref[0:1, :]), 0.0) / z
        n_ref[0, ti] = n
        r2_ref[0, ti] = r2
        e2_ref[0, ti] = jnp.exp(s2 - v2_ref[0:1, :])

        def picks(mask):
            return jnp.sum(jnp.where(mask, 1.0, 0.0), axis=0, keepdims=True)

        return (jnp.abs(picks(in1) - K) + jnp.abs(picks(r2 < float(K)) - K)
                + jnp.abs(jnp.sum(sel, axis=0, keepdims=True) - K))

    def route_block(exact):
        off = jnp.zeros((1, LANES), F32)
        for ti, t0 in enumerate(range(0, T, LANES)):
            lanes = slice(t0, t0 + LANES)
            q = q_ref[lanes, :].astype(BF16)
            s1 = _dot_nt(sk_ref[0, 0], q[:, :half])
            s2 = _dot_nt(sk_ref[0, 1], q[:, half:])
            off = off + route(s1, s2, ti, vals_ref.at[2 * ti], vals_ref.at[2 * ti + 1], exact)
        return off

    @pl.when(jnp.max(route_block(exact=False)) > 0.0)
    def _():
        route_block(exact=True)


def peer_route(qp, subkeys, tile):
    n_tok = qp.shape[0]
    heads, _, n_keys, half = subkeys.shape
    spec = pl.BlockSpec((1, tile // LANES, n_keys, LANES), lambda i, h: (h, i, 0, 0))
    shape = jax.ShapeDtypeStruct((heads, n_tok // LANES, n_keys, LANES), F32)
    return pl.pallas_call(
        _peer_route_kernel,
        grid=(n_tok // tile, heads),
        in_specs=[pl.BlockSpec((tile, 2 * half), lambda i, h: (i, h)),
                  pl.BlockSpec((1, 2, n_keys, half), lambda i, h: (h, 0, 0, 0))],
        out_specs=[spec] * 4,
        out_shape=[shape] * 4,
        scratch_shapes=[pltpu.VMEM((2 * (tile // LANES), PEER_TOPK, LANES), F32)],
        compiler_params=_params("parallel", "arbitrary"),
        name="peer_route",
    )(qp, subkeys)


def _peer_gates_kernel(e1_ref, n_ref, r2_ref, e2_ref, g_ref):
    heads, n_tiles, n_keys, _ = r2_ref.shape
    per_blk = e1_ref.shape[2]
    for ii in range(per_blk):
        for ti in range(n_tiles):
            g = jnp.zeros((n_keys, LANES), F32)
            for h in range(heads):
                n = n_ref[h, ti, ii:ii + 1, :]
                e1 = e1_ref[h, ti, ii:ii + 1, :]
                g = g + jnp.where(r2_ref[h, ti] < n, e2_ref[h, ti], 0.0) * e1
            g_ref[ii * n_keys:(ii + 1) * n_keys, ti * LANES:(ti + 1) * LANES] = g.astype(BF16)


def peer_gates(tables, tile, e_blk):
    e1, n, r2, e2 = tables
    heads, _, n_keys, _ = e1.shape
    n_tok = e1.shape[1] * LANES
    per_blk = e_blk // n_keys
    assert e_blk % (SUBLANES * n_keys) == 0
    key_spec = pl.BlockSpec((heads, tile // LANES, n_keys, LANES), lambda i, e: (0, i, 0, 0))
    blk_spec = pl.BlockSpec((heads, tile // LANES, per_blk, LANES), lambda i, e: (0, i, e, 0))
    return pl.pallas_call(
        _peer_gates_kernel,
        grid=(n_tok // tile, n_keys // per_blk),
        in_specs=[blk_spec, blk_spec, key_spec, key_spec],
        out_specs=pl.BlockSpec((e_blk, tile), lambda i, e: (e, i)),
        out_shape=jax.ShapeDtypeStruct((n_keys * n_keys, n_tok), BF16),
        compiler_params=_params("parallel", "arbitrary"),
        name="peer_gates",
    )(e1, n, r2, e2)


def _peer_dense_kernel(xt_ref, u_ref, vt_ref, g_ref, o_ref, a_ref, *, chunk):
    E = u_ref.shape[0]

    @pl.when(pl.program_id(1) == 0)
    def _():
        o_ref[...] = jnp.zeros(o_ref.shape, F32)

    for c0 in range(0, E, chunk):
        rows = slice(c0, c0 + chunk)
        pre = _dot(u_ref[rows, :], xt_ref[...])
        act = pre * (1.0 + lax.erf(pre * (2.0 ** -0.5)))
        a_ref[rows, :] = (act * g_ref[rows, :].astype(F32)).astype(BF16)
    o_ref[...] += _dot(vt_ref[...], a_ref[...])


def peer_dense(xt, u, vt, g, tile, e_blk):
    D, n_tok = xt.shape
    n_exp = u.shape[0]
    return pl.pallas_call(
        functools.partial(_peer_dense_kernel, chunk=_pick(e_blk, 256, LANES)),
        grid=(n_tok // tile, n_exp // e_blk),
        in_specs=[pl.BlockSpec((D, tile), lambda i, e: (0, i)),
                  pl.BlockSpec((e_blk, D), lambda i, e: (e, 0)),
                  pl.BlockSpec((D, e_blk), lambda i, e: (0, e)),
                  pl.BlockSpec((e_blk, tile), lambda i, e: (e, i))],
        out_specs=pl.BlockSpec((D, tile), lambda i, e: (0, i)),
        out_shape=jax.ShapeDtypeStruct((D, n_tok), F32),
        scratch_shapes=[pltpu.VMEM((e_blk, tile), BF16)],
        compiler_params=_params("parallel", "arbitrary"),
        name="peer_dense",
    )(xt, u, vt, g)


def _final_kernel(x_ref, pt_ref, g_ref, o_ref):
    o_ref[...] = _rms(x_ref[...] + pt_ref[...].T, g_ref[...])


def final_norm(x, p_t, g, tm=512):
    M, D = x.shape
    tm = _pick(M, tm, LANES)
    spec = pl.BlockSpec((tm, D), lambda i: (i, 0))
    return pl.pallas_call(
        _final_kernel,
        grid=(M // tm,),
        in_specs=[spec, pl.BlockSpec((D, tm), lambda i: (0, i)),
                  pl.BlockSpec((1, D), lambda i: (0, 0))],
        out_specs=spec,
        out_shape=jax.ShapeDtypeStruct((M, D), F32),
        compiler_params=_params("parallel"),
        name="final_norm",
    )(x, p_t, g.reshape(1, D))


def _rope_tables(pos, half):
    inv = 1.0 / (ROPE_BASE ** (jnp.arange(half, dtype=F32) / half))
    ang = pos.astype(F32)[:, None] * inv[None, :]
    return jnp.cos(ang), jnp.sin(ang)


def kernel(x_prompt, x_sample, state_pool, state_ret, cache_mem_k, cache_mem_v, mem_prompt,
           norm_mix_g, w_in, pool_w, pool_scale, ret_norm_g, w_out, norm_mem_g,
           norm_memsrc_g, w_mq, w_mk, w_mv, w_mo, norm_ffn_g, peer_wq, peer_subkeys,
           peer_u, peer_v, norm_final_g):
    Bp, Tp, D = x_prompt.shape
    Bs, Ts, _ = x_sample.shape
    depth = w_in.shape[0]
    assert depth == 1, "single-layer step"
    n_p, n_s = Bp * Tp, Bs * Ts
    W = pool_w.shape[1] * pool_w.shape[2]
    dk = W // RET_HEADS
    M = mem_prompt.shape[1]
    n_keys = peer_subkeys.shape[3]
    assert POOL_BUF >= Ts and w_in.shape[2] == 5 * W

    bf = lambda a: a.astype(BF16)
    layer = lambda a: a.reshape(a.shape[1:])
    xp, xs = x_prompt.reshape(n_p, D), x_sample.reshape(n_s, D)
    pool0 = layer(state_pool)

    w_in_b, pool_w_b = bf(layer(w_in)), bf(layer(pool_w))
    proj_p = norm_matmul(xp, norm_mix_g[0], w_in_b)
    proj_s = norm_matmul(xs, norm_mix_g[0], w_in_b)
    a_p = pool_prompt(proj_p, pool_w_b, pool_scale[0], Bp, Tp)
    p_s = proj_s[:, :W].reshape(Bs, Ts, W)
    a_s = pool_sample(pool0.transpose(1, 0, 2), p_s.transpose(1, 0, 2),
                      pool_w_b, pool_scale[0]).transpose(1, 0, 2).reshape(n_s, W)

    log_gamma = jnp.log(1.0 - 2.0 ** (-5.0 - jnp.arange(RET_HEADS, dtype=F32)))
    cos_p, sin_p = _rope_tables(jnp.arange(Tp, dtype=jnp.int32), dk // 2)
    cos_s, sin_s = _rope_tables(PAST_LEN + jnp.arange(Ts, dtype=jnp.int32), dk // 2)
    b_p, ret_p = retention_prompt(proj_p, cos_p, sin_p, ret_norm_g[0], log_gamma, Bp, Tp, W)
    b_s, ret_s = retention_sample(proj_s, layer(state_ret), cos_s, sin_s, ret_norm_g[0],
                                  log_gamma, Bs, Ts, W)

    w_o = bf(layer(w_out))
    w_o = [w_o[:W], w_o[W:]]
    x1_p = residual_matmul(xp, [a_p, b_p], w_o)
    x1_s = residual_matmul(xs, [a_s, bf(b_s)], w_o)

    mem = mem_prompt.reshape(Bp * M, D)
    mk = norm_matmul(mem, norm_memsrc_g[0], bf(layer(w_mk)))
    mv = norm_matmul(mem, norm_memsrc_g[0], bf(layer(w_mv)))
    w_mq_b, w_mo_b = bf(layer(w_mq)), bf(layer(w_mo))
    o_p = mem_attn_prompt(norm_matmul(x1_p, norm_mem_g[0], w_mq_b), mk, mv, Bp, Tp)
    o_s = mem_attn_sample(norm_matmul(x1_s, norm_mem_g[0], w_mq_b), layer(cache_mem_k),
                          layer(cache_mem_v), Bs, Ts)
    x2_p = residual_matmul(x1_p, [o_p], [w_mo_b])
    x2_s = residual_matmul(x1_s, [bf(o_s)], [w_mo_b])

    wq_b, sk_b = bf(layer(peer_wq)), bf(layer(peer_subkeys))
    u_b, vt_b = bf(layer(peer_u)), bf(layer(peer_v)).T
    e_blk = SUBLANES * n_keys

    def peer_and_norm(x2):
        qp, h = norm_matmul(x2, norm_ffn_g[0], wq_b, emit_h=True)
        tile = _pick(x2.shape[0], 512, LANES)
        gates = peer_gates(peer_route(qp, sk_b, tile), tile, e_blk)
        po_t = peer_dense(h.T, u_b, vt_b, gates, tile, e_blk)
        return final_norm(x2, po_t, norm_final_g)

    y_p, y_s = peer_and_norm(x2_p), peer_and_norm(x2_s)

    hd = D // MEM_HEADS
    new_pool_p = proj_p.reshape(Bp, Tp, 5 * W)[:, Tp - POOL_BUF:, :W]
    new_pool_s = jnp.concatenate([pool0[:, Ts:], p_s], axis=1)
    return (y_p.reshape(Bp, Tp, D),
            y_s.reshape(Bs, Ts, D),
            new_pool_p[None],
            ret_p[None],
            mk.reshape(1, Bp, M, MEM_HEADS, hd),
            mv.reshape(1, Bp, M, MEM_HEADS, hd),
            new_pool_s[None].astype(state_pool.dtype),
            ret_s[None].astype(state_ret.dtype))
```

```python
import functools

import jax
import jax.numpy as jnp
from jax import lax
from jax.experimental import pallas as pl
from jax.experimental.pallas import tpu as pltpu

F32 = jnp.float32
BF16 = jnp.bfloat16

NORM_EPS = 1e-6
PAST_LEN = 16384
ROPE_BASE = 10000.0
POOL_WINDOWS = (2, 4, 8, 16)
POOL_BUF = max(POOL_WINDOWS) - 1
POOL_PAD = POOL_BUF + 1
RET_HEADS = 4
MEM_HEADS = 4
PEER_HEADS = 8
PEER_TOPK = 16

LANES = 128
SUBLANES = 8
VMEM_LIMIT = 52 * 1024 * 1024
NEG_INF = float("-inf")


def _params(*sem):
    return pltpu.CompilerParams(dimension_semantics=sem, vmem_limit_bytes=VMEM_LIMIT)


def _pick(n, pref, mult=SUBLANES):
    for c in range(min(pref, n), 0, -1):
        if n % c == 0 and c % mult == 0:
            return c
    raise ValueError(f"no tile for {n} (pref {pref}, mult {mult})")


def _dot(a, b):
    return jnp.dot(a, b, preferred_element_type=F32)


def _dot_nt(a, b):
    return lax.dot_general(a, b, (((1,), (1,)), ((), ())), preferred_element_type=F32)


def _dot_tn(a, b):
    return lax.dot_general(a, b, (((0,), (0,)), ((), ())), preferred_element_type=F32)


def _rms(x, g):
    ms = jnp.mean(x * x, axis=-1, keepdims=True)
    return x * lax.rsqrt(ms + NORM_EPS) * g


def _norm_mm_kernel(x_ref, g_ref, w_ref, o_ref, *rest, emit_h):
    if emit_h:
        h_out_ref, h_ref = rest
    else:
        (h_ref,) = rest

    @pl.when(pl.program_id(1) == 0)
    def _():
        h = _rms(x_ref[...], g_ref[...]).astype(BF16)
        h_ref[...] = h
        if emit_h:
            h_out_ref[...] = h

    o_ref[...] = _dot(h_ref[...], w_ref[...]).astype(o_ref.dtype)


def norm_matmul(x, g, w, *, out_dtype=F32, emit_h=False, tm=1024, tn=1024):
    M, K = x.shape
    N = w.shape[1]
    tm, tn = _pick(M, tm), _pick(N, tn, LANES)
    out_shape = [jax.ShapeDtypeStruct((M, N), out_dtype)]
    out_specs = [pl.BlockSpec((tm, tn), lambda i, j: (i, j))]
    if emit_h:
        out_shape.append(jax.ShapeDtypeStruct((M, K), BF16))
        out_specs.append(pl.BlockSpec((tm, K), lambda i, j: (i, 0)))
    res = pl.pallas_call(
        functools.partial(_norm_mm_kernel, emit_h=emit_h),
        grid=(M // tm, N // tn),
        in_specs=[pl.BlockSpec((tm, K), lambda i, j: (i, 0)),
                  pl.BlockSpec((1, K), lambda i, j: (0, 0)),
                  pl.BlockSpec((K, tn), lambda i, j: (0, j))],
        out_specs=out_specs,
        out_shape=out_shape,
        scratch_shapes=[pltpu.VMEM((tm, K), BF16)],
        compiler_params=_params("parallel", "arbitrary"),
        name="norm_matmul",
    )(x, g.reshape(1, K), w)
    return res if emit_h else res[0]


def _res_mm_kernel(*refs, n_lhs):
    res_ref = refs[0]
    a_refs = refs[1:1 + n_lhs]
    w_refs = refs[1 + n_lhs:1 + 2 * n_lhs]
    o_ref = refs[1 + 2 * n_lhs]
    acc = res_ref[...]
    for a_ref, w_ref in zip(a_refs, w_refs):
        acc = acc + _dot(a_ref[...], w_ref[...])
    o_ref[...] = acc


def residual_matmul(res, lhs, ws, *, tm=1024, tn=1024):
    M, N = res.shape
    tm, tn = _pick(M, tm), _pick(N, tn, LANES)
    n = len(lhs)
    in_specs = [pl.BlockSpec((tm, tn), lambda i, j: (i, j))]
    in_specs += [pl.BlockSpec((tm, a.shape[1]), lambda i, j: (i, 0)) for a in lhs]
    in_specs += [pl.BlockSpec((w.shape[0], tn), lambda i, j: (0, j)) for w in ws]
    return pl.pallas_call(
        functools.partial(_res_mm_kernel, n_lhs=n),
        grid=(M // tm, N // tn),
        in_specs=in_specs,
        out_specs=pl.BlockSpec((tm, tn), lambda i, j: (i, j)),
        out_shape=jax.ShapeDtypeStruct((M, N), F32),
        compiler_params=_params("parallel", "arbitrary"),
        name="residual_matmul",
    )(res, *lhs, *ws)


def _pool_prompt_kernel(p_ref, pw_ref, sc_ref, a_ref, ext_ref, *, rows):
    T, GD = p_ref.shape
    g = pl.program_id(1)
    ext_ref[0:POOL_PAD, :] = jnp.zeros((POOL_PAD, GD), F32)
    ext_ref[POOL_PAD:, :] = p_ref[...]
    for gi, w in enumerate(POOL_WINDOWS):
        @pl.when(g == gi)
        def _(w=w):
            for r0 in range(0, T, rows):
                acc = ext_ref[POOL_PAD + r0:POOL_PAD + r0 + rows, :]
                tok = acc
                for s in range(1, w):
                    acc = acc + ext_ref[POOL_PAD + r0 - s:POOL_PAD + r0 - s + rows, :]
                t = lax.broadcasted_iota(jnp.int32, (rows, GD), 0) + r0
                cnt = jnp.minimum(t + 1, w).astype(F32)
                d = acc / cnt - tok
                y = _dot(d.astype(BF16), pw_ref[0]) * sc_ref[...]
                a_ref[r0:r0 + rows, :] = y.astype(BF16)


def pool_prompt(proj, pool_w, pool_scale, n_seq, T):
    G, GD, _ = pool_w.shape
    W = G * GD
    return pl.pallas_call(
        functools.partial(_pool_prompt_kernel, rows=_pick(T, 256)),
        grid=(n_seq, G),
        in_specs=[pl.BlockSpec((T, GD), lambda b, g: (b, g)),
                  pl.BlockSpec((1, GD, GD), lambda b, g: (g, 0, 0)),
                  pl.BlockSpec((1, GD), lambda b, g: (0, g))],
        out_specs=pl.BlockSpec((T, GD), lambda b, g: (b, g)),
        out_shape=jax.ShapeDtypeStruct((n_seq * T, W), BF16),
        scratch_shapes=[pltpu.VMEM((POOL_PAD + T, GD), F32)],
        compiler_params=_params("parallel", "arbitrary"),
        name="pool_prompt",
    )(proj, pool_w, pool_scale.reshape(1, W))


def _pool_sample_kernel(pre_ref, p_ref, pw_ref, sc_ref, a_ref):
    Ts = p_ref.shape[0]
    g = pl.program_id(0)

    def ext(r):
        return pre_ref[r] if r < POOL_BUF else p_ref[r - POOL_BUF]

    for gi, w in enumerate(POOL_WINDOWS):
        @pl.when(g == gi)
        def _(w=w):
            for t in range(Ts):
                acc = ext(POOL_BUF + t)
                tok = acc
                for s in range(1, w):
                    acc = acc + ext(POOL_BUF + t - s)
                cnt = float(min(PAST_LEN + t + 1, w))
                d = acc / cnt - tok
                y = _dot(d.astype(BF16), pw_ref[0]) * sc_ref[...]
                a_ref[t] = y.astype(BF16)


def pool_sample(pre_t, p_t, pool_w, pool_scale):
    G, GD, _ = pool_w.shape
    W = G * GD
    Ts, B, _ = p_t.shape
    return pl.pallas_call(
        _pool_sample_kernel,
        grid=(G,),
        in_specs=[pl.BlockSpec((POOL_BUF, B, GD), lambda g: (0, 0, g)),
                  pl.BlockSpec((Ts, B, GD), lambda g: (0, 0, g)),
                  pl.BlockSpec((1, GD, GD), lambda g: (g, 0, 0)),
                  pl.BlockSpec((1, GD), lambda g: (0, g))],
        out_specs=pl.BlockSpec((Ts, B, GD), lambda g: (0, 0, g)),
        out_shape=jax.ShapeDtypeStruct((Ts, B, W), BF16),
        compiler_params=_params("arbitrary"),
        name="pool_sample",
    )(pre_t, p_t, pool_w, pool_scale.reshape(1, W))


def _rotary(x, cos, sin):
    half = x.shape[-1] // 2
    x1, x2 = x[:, :half], x[:, half:]
    return jnp.concatenate([x1 * cos - x2 * sin, x1 * sin + x2 * cos], axis=-1)


def _retention_chunk(q, k, v, gate, cos, sin, gn, s, lg):
    C, dk = q.shape
    qr = _rotary(q, cos, sin).astype(BF16)
    kr = _rotary(k, cos, sin) * (dk ** -0.5)
    vb = v.astype(BF16)
    ri = lax.broadcasted_iota(jnp.int32, (C, C), 0)
    ci = lax.broadcasted_iota(jnp.int32, (C, C), 1)
    diff = (ri - ci).astype(F32)
    decay = jnp.where(diff >= 0, jnp.exp(lg * jnp.maximum(diff, 0.0)), 0.0)
    inner = _dot_nt(qr, kr.astype(BF16)) * decay
    idx = lax.broadcasted_iota(jnp.int32, (C, dk), 0).astype(F32)
    xi = jnp.exp(lg * (idx + 1.0))
    zeta = jnp.exp(lg * (C - 1.0 - idx))
    o = _dot(inner.astype(BF16), vb) + _dot(qr, s.astype(BF16)) * xi
    g_c = jnp.exp(jnp.full((1, s.shape[1]), C, F32) * lg)
    s_new = s * g_c + _dot_tn((kr * zeta).astype(BF16), vb)
    mu = jnp.mean(o, axis=-1, keepdims=True)
    oc = o - mu
    var = jnp.mean(oc * oc, axis=-1, keepdims=True)
    on = oc * lax.rsqrt(var + NORM_EPS) * gn
    return gate * jax.nn.sigmoid(gate) * on, s_new


def _ret_prompt_kernel(lg_ref, q_ref, k_ref, v_ref, gate_ref, cos_ref, sin_ref, gn_ref,
                       b_ref, s_out_ref, s_ref):
    h = pl.program_id(1)
    c = pl.program_id(2)

    @pl.when(c == 0)
    def _():
        s_ref[...] = jnp.zeros(s_ref.shape, F32)

    out, s_new = _retention_chunk(q_ref[...], k_ref[...], v_ref[...], gate_ref[...],
                                  cos_ref[...], sin_ref[...], gn_ref[...], s_ref[...],
                                  lg_ref[h])
    b_ref[...] = out.astype(BF16)
    s_ref[...] = s_new

    @pl.when(c == pl.num_programs(2) - 1)
    def _():
        s_out_ref[0, 0] = s_new


def retention_prompt(proj, cos, sin, ret_norm_g, log_gamma, n_seq, T, width):
    H = RET_HEADS
    dk = width // H
    C = _pick(T, 256)
    nC = T // C

    def col(part):
        return lambda b, h, c: (b * nC + c, part * H + h)

    return pl.pallas_call(
        _ret_prompt_kernel,
        grid=(n_seq, H, nC),
        in_specs=[pl.BlockSpec(memory_space=pltpu.SMEM),
                  pl.BlockSpec((C, dk), col(1)),
                  pl.BlockSpec((C, dk), col(2)),
                  pl.BlockSpec((C, dk), col(3)),
                  pl.BlockSpec((C, dk), col(4)),
                  pl.BlockSpec((C, dk // 2), lambda b, h, c: (c, 0)),
                  pl.BlockSpec((C, dk // 2), lambda b, h, c: (c, 0)),
                  pl.BlockSpec((1, dk), lambda b, h, c: (0, h))],
        out_specs=[pl.BlockSpec((C, dk), lambda b, h, c: (b * nC + c, h)),
                   pl.BlockSpec((1, 1, dk, dk), lambda b, h, c: (b, h, 0, 0))],
        out_shape=[jax.ShapeDtypeStruct((n_seq * T, width), BF16),
                   jax.ShapeDtypeStruct((n_seq, H, dk, dk), F32)],
        scratch_shapes=[pltpu.VMEM((dk, dk), F32)],
        compiler_params=_params("parallel", "parallel", "arbitrary"),
        name="retention_prompt",
    )(log_gamma, proj, proj, proj, proj, cos, sin, ret_norm_g.reshape(1, width))


def _ret_sample_kernel(lg_ref, q_ref, k_ref, v_ref, gate_ref, cos_ref, sin_ref, gn_ref,
                       s_in_ref, b_ref, s_out_ref, *, n_seq, Ts):
    H = RET_HEADS
    dk = q_ref.shape[1] // H
    cos, sin = cos_ref[...], sin_ref[...]
    for b in range(n_seq):
        rows = slice(b * Ts, (b + 1) * Ts)
        for h in range(H):
            cols = slice(h * dk, (h + 1) * dk)
            out, s_new = _retention_chunk(q_ref[rows, cols], k_ref[rows, cols],
                                          v_ref[rows, cols], gate_ref[rows, cols],
                                          cos, sin, gn_ref[:, cols], s_in_ref[b, h],
                                          lg_ref[h])
            b_ref[rows, cols] = out
            s_out_ref[b, h] = s_new


def retention_sample(proj, state, cos, sin, ret_norm_g, log_gamma, n_seq, Ts, width):
    H = RET_HEADS
    dk = width // H
    bb = _pick(n_seq, 4, 1)
    rows = bb * Ts

    def col(part):
        return lambda i: (i, part)

    return pl.pallas_call(
        functools.partial(_ret_sample_kernel, n_seq=bb, Ts=Ts),
        grid=(n_seq // bb,),
        in_specs=[pl.BlockSpec(memory_space=pltpu.SMEM),
                  pl.BlockSpec((rows, width), col(1)),
                  pl.BlockSpec((rows, width), col(2)),
                  pl.BlockSpec((rows, width), col(3)),
                  pl.BlockSpec((rows, width), col(4)),
                  pl.BlockSpec((Ts, dk // 2), lambda i: (0, 0)),
                  pl.BlockSpec((Ts, dk // 2), lambda i: (0, 0)),
                  pl.BlockSpec((1, width), lambda i: (0, 0)),
                  pl.BlockSpec((bb, H, dk, dk), lambda i: (i, 0, 0, 0))],
        out_specs=[pl.BlockSpec((rows, width), lambda i: (i, 0)),
                   pl.BlockSpec((bb, H, dk, dk), lambda i: (i, 0, 0, 0))],
        out_shape=[jax.ShapeDtypeStruct((n_seq * Ts, width), F32),
                   jax.ShapeDtypeStruct((n_seq, H, dk, dk), F32)],
        compiler_params=_params("parallel"),
        name="retention_sample",
    )(log_gamma, proj, proj, proj, proj, cos, sin, ret_norm_g.reshape(1, width), state)


def _attend(q, k, v):
    s = _dot_nt(q.astype(BF16), k.astype(BF16)) * (q.shape[-1] ** -0.5)
    e = jnp.exp(s - jnp.max(s, axis=-1, keepdims=True))
    p = e / jnp.sum(e, axis=-1, keepdims=True)
    return _dot(p.astype(BF16), v.astype(BF16))


def _attn_prompt_kernel(q_ref, k_ref, v_ref, o_ref):
    o_ref[...] = _attend(q_ref[...], k_ref[...], v_ref[...]).astype(BF16)


def mem_attn_prompt(q, mk, mv, n_seq, T):
    D = mk.shape[1]
    M = mk.shape[0] // n_seq
    hd = D // MEM_HEADS
    tq = _pick(T, 512)
    nq = T // tq
    return pl.pallas_call(
        _attn_prompt_kernel,
        grid=(n_seq, nq, MEM_HEADS),
        in_specs=[pl.BlockSpec((tq, hd), lambda b, i, h: (b * nq + i, h)),
                  pl.BlockSpec((M, hd), lambda b, i, h: (b, h)),
                  pl.BlockSpec((M, hd), lambda b, i, h: (b, h))],
        out_specs=pl.BlockSpec((tq, hd), lambda b, i, h: (b * nq + i, h)),
        out_shape=jax.ShapeDtypeStruct((n_seq * T, D), BF16),
        compiler_params=_params("parallel", "parallel", "arbitrary"),
        name="mem_attn_prompt",
    )(q, mk, mv)


def _attn_sample_kernel(q_ref, k_ref, v_ref, o_ref, *, n_seq, Ts):
    _, M, H, hd = k_ref.shape
    col_head = lax.broadcasted_iota(jnp.int32, (H * Ts, M * H), 1) % H
    row_head = lax.broadcasted_iota(jnp.int32, (H * Ts, M * H), 0) // Ts
    own = col_head == row_head
    for b in range(n_seq):
        rows = slice(b * Ts, (b + 1) * Ts)
        q = jnp.concatenate([q_ref[rows, h * hd:(h + 1) * hd] for h in range(H)], axis=0)
        k = k_ref[b].reshape(M * H, hd).astype(BF16)
        v = v_ref[b].reshape(M * H, hd).astype(BF16)
        s = jnp.where(own, _dot_nt(q.astype(BF16), k) * (hd ** -0.5), NEG_INF)
        e = jnp.exp(s - jnp.max(s, axis=-1, keepdims=True))
        p = e / jnp.sum(e, axis=-1, keepdims=True)
        o = _dot(p.astype(BF16), v)
        for h in range(H):
            o_ref[rows, h * hd:(h + 1) * hd] = o[h * Ts:(h + 1) * Ts, :]


def mem_attn_sample(q, ck, cv, n_seq, Ts):
    _, M, H, hd = ck.shape
    D = H * hd
    bb = _pick(n_seq, 2, 1)
    rows = bb * Ts
    return pl.pallas_call(
        functools.partial(_attn_sample_kernel, n_seq=bb, Ts=Ts),
        grid=(n_seq // bb,),
        in_specs=[pl.BlockSpec((rows, D), lambda i: (i, 0)),
                  pl.BlockSpec((bb, M, H, hd), lambda i: (i, 0, 0, 0)),
                  pl.BlockSpec((bb, M, H, hd), lambda i: (i, 0, 0, 0))],
        out_specs=pl.BlockSpec((rows, D), lambda i: (i, 0)),
        out_shape=jax.ShapeDtypeStruct((n_seq * Ts, D), F32),
        compiler_params=_params("parallel"),
        name="mem_attn_sample",
    )(q, ck, cv)


def _extract16(cur, idx, exact):
    for a in range(PEER_TOPK):
        m = jnp.max(cur, axis=0, keepdims=True)
        hit = cur == m
        if exact:
            first = jnp.min(jnp.where(hit, idx, float(cur.shape[0] * PEER_TOPK)),
                            axis=0, keepdims=True)
            hit = idx == first
        cur = jnp.where(hit, NEG_INF, cur)
        yield a, m, hit


def _top16(s, vals_ref, exact):
    kio = lax.broadcasted_iota(jnp.int32, s.shape, 0).astype(F32)
    rank = jnp.full(s.shape, float(PEER_TOPK), F32)
    for a, m, hit in _extract16(s, kio, exact):
        rank = jnp.where(hit, float(a), rank)
        vals_ref[a:a + 1, :] = m
    return rank


def _candidate_grid(v1_ref, v2_ref):
    K = PEER_TOPK
    L = v1_ref.shape[1]
    v2_lo = v2_ref[0:SUBLANES, :]
    bio = lax.broadcasted_iota(jnp.int32, (SUBLANES, L), 0)
    bio_f = bio.astype(F32)
    cands = [v1_ref[0:1, :] + v2_ref[...]]
    cidx = [lax.broadcasted_iota(jnp.int32, (K, L), 0).astype(F32)]
    pieces = [(0, K, 0)]
    row = K
    for a in range(1, SUBLANES):
        limit = K // (a + 1)
        c = v1_ref[a:a + 1, :] + v2_lo
        if limit < SUBLANES:
            c = jnp.where(bio < limit, c, NEG_INF)
        cands.append(c)
        cidx.append(bio_f + float(a * K))
        pieces.append((row, SUBLANES, a))
        row += SUBLANES
    cands.append(v1_ref[SUBLANES:K, :] + v2_ref[0:1, :])
    cidx.append((bio_f + float(SUBLANES)) * float(K))
    pieces.append((row, SUBLANES, None))
    return jnp.concatenate(cands, axis=0), jnp.concatenate(cidx, axis=0), pieces


def _peer_route_kernel(q_ref, sk_ref, e1_ref, n_ref, r2_ref, e2_ref, vals_ref):
    T = q_ref.shape[0]
    half = sk_ref.shape[3]
    K = PEER_TOPK

    def route(s1, s2, ti, v1_ref, v2_ref, exact):
        r1 = _top16(s1, v1_ref, exact)
        r2 = _top16(s2, v2_ref, exact)
        cand, cidx, pieces = _candidate_grid(v1_ref, v2_ref)
        sel = jnp.zeros(cand.shape, F32)
        for _, _, hit in _extract16(cand, cidx, exact):
            sel = jnp.where(hit, 1.0, sel)

        top = v1_ref[0:1, :] + v2_ref[0:1, :]
        z = jnp.sum(sel * jnp.exp(cand - top), axis=0, keepdims=True)
        n = jnp.zeros(s1.shape, F32)
        for row0, rows, a in pieces:
            piece = sel[row0:row0 + rows, :]
            if a is None:
                for r in range(rows):
                    n = jnp.where(r1 == float(SUBLANES + r), piece[r:r + 1, :], n)
            else:
                n = jnp.where(r1 == float(a), jnp.sum(piece, axis=0, keepdims=True), n)

        in1 = r1 < float(K)
        e1_ref[0, ti] = 0.5 * jnp.where(in1, jnp.exp(s1 - v1_ref[0:1, :]), 0.0) / z
        n_ref[0, ti] = n
        r2_ref[0, ti] = r2
        e2_ref[0, ti] = jnp.exp(s2 - v2_ref[0:1, :])

        def picks(mask):
            return jnp.sum(jnp.where(mask, 1.0, 0.0), axis=0, keepdims=True)

        return (jnp.abs(picks(in1) - K) + jnp.abs(picks(r2 < float(K)) - K)
                + jnp.abs(jnp.sum(sel, axis=0, keepdims=True) - K))

    def route_block(exact):
        off = jnp.zeros((1, LANES), F32)
        for ti, t0 in enumerate(range(0, T, LANES)):
            lanes = slice(t0, t0 + LANES)
            q = q_ref[lanes, :].astype(BF16)
            s1 = _dot_nt(sk_ref[0, 0], q[:, :half])
            s2 = _dot_nt(sk_ref[0, 1], q[:, half:])
            off = off + route(s1, s2, ti, vals_ref.at[2 * ti], vals_ref.at[2 * ti + 1], exact)
        return off

    @pl.when(jnp.max(route_block(exact=False)) > 0.0)
    def _():
        route_block(exact=True)


def peer_route(qp, subkeys, tile):
    n_tok = qp.shape[0]
    heads, _, n_keys, half = subkeys.shape
    spec = pl.BlockSpec((1, tile // LANES, n_keys, LANES), lambda i, h: (h, i, 0, 0))
    shape = jax.ShapeDtypeStruct((heads, n_tok // LANES, n_keys, LANES), F32)
    return pl.pallas_call(
        _peer_route_kernel,
        grid=(n_tok // tile, heads),
        in_specs=[pl.BlockSpec((tile, 2 * half), lambda i, h: (i, h)),
                  pl.BlockSpec((1, 2, n_keys, half), lambda i, h: (h, 0, 0, 0))],
        out_specs=[spec] * 4,
        out_shape=[shape] * 4,
        scratch_shapes=[pltpu.VMEM((2 * (tile // LANES), PEER_TOPK, LANES), F32)],
        compiler_params=_params("parallel", "arbitrary"),
        name="peer_route",
    )(qp, subkeys)


def _peer_gates_kernel(e1_ref, n_ref, r2_ref, e2_ref, g_ref):
    heads, n_tiles, n_keys, _ = r2_ref.shape
    per_blk = e1_ref.shape[2]
    for ii in range(per_blk):
        for ti in range(n_tiles):
            g = jnp.zeros((n_keys, LANES), F32)
            for h in range(heads):
                n = n_ref[h, ti, ii:ii + 1, :]
                e1 = e1_ref[h, ti, ii:ii + 1, :]
                g = g + jnp.where(r2_ref[h, ti] < n, e2_ref[h, ti], 0.0) * e1
            g_ref[ii * n_keys:(ii + 1) * n_keys, ti * LANES:(ti + 1) * LANES] = g.astype(BF16)


def peer_gates(tables, tile, e_blk):
    e1, n, r2, e2 = tables
    heads, _, n_keys, _ = e1.shape
    n_tok = e1.shape[1] * LANES
    per_blk = e_blk // n_keys
    assert e_blk % (SUBLANES * n_keys) == 0
    key_spec = pl.BlockSpec((heads, tile // LANES, n_keys, LANES), lambda i, e: (0, i, 0, 0))
    blk_spec = pl.BlockSpec((heads, tile // LANES, per_blk, LANES), lambda i, e: (0, i, e, 0))
    return pl.pallas_call(
        _peer_gates_kernel,
        grid=(n_tok // tile, n_keys // per_blk),
        in_specs=[blk_spec, blk_spec, key_spec, key_spec],
        out_specs=pl.BlockSpec((e_blk, tile), lambda i, e: (e, i)),
        out_shape=jax.ShapeDtypeStruct((n_keys * n_keys, n_tok), BF16),
        compiler_params=_params("parallel", "arbitrary"),
        name="peer_gates",
    )(e1, n, r2, e2)


def _peer_dense_kernel(xt_ref, u_ref, vt_ref, g_ref, o_ref, a_ref, *, chunk):
    E = u_ref.shape[0]

    @pl.when(pl.program_id(1) == 0)
    def _():
        o_ref[...] = jnp.zeros(o_ref.shape, F32)

    for c0 in range(0, E, chunk):
        rows = slice(c0, c0 + chunk)
        pre = _dot(u_ref[rows, :], xt_ref[...])
        act = pre * (1.0 + lax.erf(pre * (2.0 ** -0.5)))
        a_ref[rows, :] = (act * g_ref[rows, :].astype(F32)).astype(BF16)
    o_ref[...] += _dot(vt_ref[...], a_ref[...])


def peer_dense(xt, u, vt, g, tile, e_blk):
    D, n_tok = xt.shape
    n_exp = u.shape[0]
    return pl.pallas_call(
        functools.partial(_peer_dense_kernel, chunk=_pick(e_blk, 256, LANES)),
        grid=(n_tok // tile, n_exp // e_blk),
        in_specs=[pl.BlockSpec((D, tile), lambda i, e: (0, i)),
                  pl.BlockSpec((e_blk, D), lambda i, e: (e, 0)),
                  pl.BlockSpec((D, e_blk), lambda i, e: (0, e)),
                  pl.BlockSpec((e_blk, tile), lambda i, e: (e, i))],
        out_specs=pl.BlockSpec((D, tile), lambda i, e: (0, i)),
        out_shape=jax.ShapeDtypeStruct((D, n_tok), F32),
        scratch_shapes=[pltpu.VMEM((e_blk, tile), BF16)],
        compiler_params=_params("parallel", "arbitrary"),
        name="peer_dense",
    )(xt, u, vt, g)


def _final_kernel(x_ref, pt_ref, g_ref, o_ref):
    o_ref[...] = _rms(x_ref[...] + pt_ref[...].T, g_ref[...])


def final_norm(x, p_t, g, tm=512):
    M, D = x.shape
    tm = _pick(M, tm, LANES)
    spec = pl.BlockSpec((tm, D), lambda i: (i, 0))
    return pl.pallas_call(
        _final_kernel,
        grid=(M // tm,),
        in_specs=[spec, pl.BlockSpec((D, tm), lambda i: (0, i)),
                  pl.BlockSpec((1, D), lambda i: (0, 0))],
        out_specs=spec,
        out_shape=jax.ShapeDtypeStruct((M, D), F32),
        compiler_params=_params("parallel"),
        name="final_norm",
    )(x, p_t, g.reshape(1, D))


def _rope_tables(pos, half):
    inv = 1.0 / (ROPE_BASE ** (jnp.arange(half, dtype=F32) / half))
    ang = pos.astype(F32)[:, None] * inv[None, :]
    return jnp.cos(ang), jnp.sin(ang)


def kernel(x_prompt, x_sample, state_pool, state_ret, cache_mem_k, cache_mem_v, mem_prompt,
           norm_mix_g, w_in, pool_w, pool_scale, ret_norm_g, w_out, norm_mem_g,
           norm_memsrc_g, w_mq, w_mk, w_mv, w_mo, norm_ffn_g, peer_wq, peer_subkeys,
           peer_u, peer_v, norm_final_g):
    Bp, Tp, D = x_prompt.shape
    Bs, Ts, _ = x_sample.shape
    depth = w_in.shape[0]
    assert depth == 1, "single-layer step"
    n_p, n_s = Bp * Tp, Bs * Ts
    W = pool_w.shape[1] * pool_w.shape[2]
    dk = W // RET_HEADS
    M = mem_prompt.shape[1]
    n_keys = peer_subkeys.shape[3]
    assert POOL_BUF >= Ts and w_in.shape[2] == 5 * W

    bf = lambda a: a.astype(BF16)
    layer = lambda a: a.reshape(a.shape[1:])
    xp, xs = x_prompt.reshape(n_p, D), x_sample.reshape(n_s, D)
    pool0 = layer(state_pool)

    w_in_b, pool_w_b = bf(layer(w_in)), bf(layer(pool_w))
    proj_p = norm_matmul(xp, norm_mix_g[0], w_in_b)
    proj_s = norm_matmul(xs, norm_mix_g[0], w_in_b)
    a_p = pool_prompt(proj_p, pool_w_b, pool_scale[0], Bp, Tp)
    p_s = proj_s[:, :W].reshape(Bs, Ts, W)
    a_s = pool_sample(pool0.transpose(1, 0, 2), p_s.transpose(1, 0, 2),
                      pool_w_b, pool_scale[0]).transpose(1, 0, 2).reshape(n_s, W)

    log_gamma = jnp.log(1.0 - 2.0 ** (-5.0 - jnp.arange(RET_HEADS, dtype=F32)))
    cos_p, sin_p = _rope_tables(jnp.arange(Tp, dtype=jnp.int32), dk // 2)
    cos_s, sin_s = _rope_tables(PAST_LEN + jnp.arange(Ts, dtype=jnp.int32), dk // 2)
    b_p, ret_p = retention_prompt(proj_p, cos_p, sin_p, ret_norm_g[0], log_gamma, Bp, Tp, W)
    b_s, ret_s = retention_sample(proj_s, layer(state_ret), cos_s, sin_s, ret_norm_g[0],
                                  log_gamma, Bs, Ts, W)

    w_o = bf(layer(w_out))
    w_o = [w_o[:W], w_o[W:]]
    x1_p = residual_matmul(xp, [a_p, b_p], w_o)
    x1_s = residual_matmul(xs, [a_s, bf(b_s)], w_o)

    mem = mem_prompt.reshape(Bp * M, D)
    mk = norm_matmul(mem, norm_memsrc_g[0], bf(layer(w_mk)))
    mv = norm_matmul(mem, norm_memsrc_g[0], bf(layer(w_mv)))
    w_mq_b, w_mo_b = bf(layer(w_mq)), bf(layer(w_mo))
    o_p = mem_attn_prompt(norm_matmul(x1_p, norm_mem_g[0], w_mq_b), mk, mv, Bp, Tp)
    o_s = mem_attn_sample(norm_matmul(x1_s, norm_mem_g[0], w_mq_b), layer(cache_mem_k),
                          layer(cache_mem_v), Bs, Ts)
    x2_p = residual_matmul(x1_p, [o_p], [w_mo_b])
    x2_s = residual_matmul(x1_s, [bf(o_s)], [w_mo_b])

    wq_b, sk_b = bf(layer(peer_wq)), bf(layer(peer_subkeys))
    u_b, vt_b = bf(layer(peer_u)), bf(layer(peer_v)).T
    e_blk = SUBLANES * n_keys

    def peer_and_norm(x2):
        qp, h = norm_matmul(x2, norm_ffn_g[0], wq_b, emit_h=True)
        tile = _pick(x2.shape[0], 512, LANES)
        tables = peer_route(qp, sk_b, _pick(x2.shape[0], 2 * tile, LANES))
        gates = peer_gates(tables, tile, 4 * e_blk)
        po_t = peer_dense(h.T, u_b, vt_b, gates, tile, e_blk)
        return final_norm(x2, po_t, norm_final_g)

    y_p, y_s = peer_and_norm(x2_p), peer_and_norm(x2_s)

    hd = D // MEM_HEADS
    new_pool_p = proj_p.reshape(Bp, Tp, 5 * W)[:, Tp - POOL_BUF:, :W]
    new_pool_s = jnp.concatenate([pool0[:, Ts:], p_s], axis=1)
    return (y_p.reshape(Bp, Tp, D),
            y_s.reshape(Bs, Ts, D),
            new_pool_p[None],
            ret_p[None],
            mk.reshape(1, Bp, M, MEM_HEADS, hd),
            mv.reshape(1, Bp, M, MEM_HEADS, hd),
            new_pool_s[None].astype(state_pool.dtype),
            ret_s[None].astype(state_ret.dtype))
```
